```python
import math
import numpy as np
import jax
import jax.numpy as jnp
from jax import lax

D_MODEL = 1024
BATCH = 4
SEQ = 8192
DEPTH = 2

GRID_W = 64
CTX_LEN = 256
CHUNK = 64
S5_W = D_MODEL // 4
S5_HC = 16
S5_G = S5_W // S5_HC
S5_P = 64
GLA_W = 3 * D_MODEL // 8
GLA_H = 4
GLA_DV = GLA_W // GLA_H
GLA_DK = GLA_DV // 2
GLA_RANK = 16
GLA_TAU = 16.0
ML_W = 3 * D_MODEL // 8
ML_H = 4
ML_D = ML_W // ML_H
D_FF = ((8 * D_MODEL // 3 + 127) // 128) * 128
ALPHA = (2.0 * DEPTH) ** 0.25
BETA = (8.0 * DEPTH) ** -0.25
LN_EPS = 1e-5
IN_SPLITS = (S5_W, GLA_H * GLA_DK, GLA_H * GLA_DK, GLA_W, GLA_W, 2 * GLA_RANK, ML_W, ML_W, ML_W, ML_W, 2 * ML_H, 2 * ML_H)
D_IN = sum(IN_SPLITS)
IN_OFFSETS = tuple(int(o) for o in np.cumsum(IN_SPLITS)[:-1])

kernel_name = 'hybrid_s5_gla_mlstm_prefix_dit'


def _ln(x):
    xf = x.astype(jnp.float32)
    mu = jnp.mean(xf, axis=-1, keepdims=True)
    var = jnp.mean(jnp.square(xf - mu), axis=-1, keepdims=True)
    return (xf - mu) * lax.rsqrt(var + LN_EPS)


def _modulate(h, shift, scale):
    return (_ln(h) * (1.0 + scale) + shift).astype(h.dtype)


def _post_ln(y, g, b):
    return (_ln(y) * g + b).astype(y.dtype)


def _identity(t):
    return t


def _flip(t):
    return jnp.flip(t, axis=1)


def _bidir(run, ctx_in, lat_in, state0, ctx_out):
    y_x, y_c = None, None
    for d, fl in enumerate((_identity, _flip)):
        yc, st = run(tuple(fl(t) for t in ctx_in[d]), d, state0, ctx_out)
        yx, _ = run(tuple(fl(t) for t in lat_in[d]), d, st, True)
        y_x = fl(yx) if y_x is None else y_x + fl(yx)
        if ctx_out:
            y_c = fl(yc) if y_c is None else y_c + fl(yc)
    return y_x, y_c


def _cplx_combine(e1, e2):
    a1r, a1i, b1r, b1i = e1
    a2r, a2i, b2r, b2i = e2
    ar = a2r * a1r - a2i * a1i
    ai = a2r * a1i + a2i * a1r
    br = a2r * b1r - a2i * b1i + b2r
    bi = a2r * b1i + a2i * b1r + b2i
    return ar, ai, br, bi


def _s5_scan(u, a_re, a_im, log_dt, b_re, b_im, s0):
    dt = jnp.exp(log_dt)[:, None]
    mag = jnp.exp(a_re * dt)
    lam_re = mag * jnp.cos(a_im * dt)
    lam_im = mag * jnp.sin(a_im * dt)
    den = a_re * a_re + a_im * a_im
    z_re = ((lam_re - 1.0) * a_re + lam_im * a_im) / den
    z_im = (lam_im * a_re - (lam_re - 1.0) * a_im) / den
    bb_re = z_re[..., None] * b_re - z_im[..., None] * b_im
    bb_im = z_re[..., None] * b_im + z_im[..., None] * b_re
    bu_re = jnp.einsum('gph,blgh->blgp', bb_re, u)
    bu_im = jnp.einsum('gph,blgh->blgp', bb_im, u)
    s_re, s_im = s0
    bu_re = bu_re.at[:, 0].add(lam_re * s_re - lam_im * s_im)
    bu_im = bu_im.at[:, 0].add(lam_re * s_im + lam_im * s_re)
    la_re = jnp.broadcast_to(lam_re, bu_re.shape)
    la_im = jnp.broadcast_to(lam_im, bu_im.shape)
    _, _, x_re, x_im = lax.associative_scan(_cplx_combine, (la_re, la_im, bu_re, bu_im), axis=1)
    return x_re, x_im


def _s5_mixer(ux, uc, a_re, a_im, log_dt, b_re, b_im, c_re, c_im, d_skip, w_glu, b_glu, ctx_out):
    f32 = jnp.float32
    a_re, a_im, log_dt, b_re, b_im, c_re, c_im, d_skip = (
        t.astype(f32) for t in (a_re, a_im, log_dt, b_re, b_im, c_re, c_im, d_skip))

    def grp(u):
        return u.astype(f32).reshape(u.shape[0], u.shape[1], S5_G, S5_HC)

    gx, gc = grp(ux), grp(uc)

    def run(inp, d, s0, want_out):
        (u,) = inp
        x_re, x_im = _s5_scan(u, a_re[d], a_im[d], log_dt[d], b_re[d], b_im[d], s0)
        final = (x_re[:, -1], x_im[:, -1])
        if not want_out:
            return None, final
        y = jnp.einsum('ghp,blgp->blgh', c_re[d], x_re) - jnp.einsum('ghp,blgp->blgh', c_im[d], x_im)
        return y, final

    zeros = jnp.zeros((gc.shape[0], S5_G, S5_P), f32)
    yx, yc = _bidir(run, [(gc,), (gc,)], [(gx,), (gx,)], (zeros, zeros), ctx_out)

    def post(y, u):
        B_, L = y.shape[:2]
        y = jax.nn.gelu((y + d_skip.reshape(S5_G, S5_HC) * u).reshape(B_, L, S5_W))
        return y * jax.nn.sigmoid(y @ w_glu + b_glu)

    return post(yx, gx), (post(yc, gc) if ctx_out else None)


def _gla_dir(q, k, v, loga, S0, want_out):
    B_, L, H, DK = q.shape
    DV = v.shape[-1]
    N = L // CHUNK
    q, k, loga = (t.reshape(B_, N, CHUNK, H, DK) for t in (q, k, loga))
    v = v.reshape(B_, N, CHUNK, H, DV)
    b = jnp.cumsum(loga, axis=2)
    b_last = b[:, :, -1]
    dS = jnp.einsum('bnshk,bnshv->bnhkv', k * jnp.exp(b_last[:, :, None] - b), v)

    def step(S, inp):
        decay, ds = inp
        return decay[..., None] * S + ds, S

    S_final, S_enter = lax.scan(step, S0, (jnp.moveaxis(jnp.exp(b_last), 1, 0), jnp.moveaxis(dS, 1, 0)))
    if not want_out:
        return None, S_final
    S_enter = jnp.moveaxis(S_enter, 0, 1)
    qd = q * jnp.exp(b)
    causal = jnp.tril(jnp.ones((CHUNK, CHUNK), dtype=bool))
    att = jnp.einsum('bnthk,bnshk->bnhts', qd, k * jnp.exp(-b))
    att = jnp.where(causal, att, 0.0)
    o = jnp.einsum('bnthk,bnhkv->bnthv', qd, S_enter) + jnp.einsum('bnhts,bnshv->bnthv', att, v)
    return o.reshape(B_, L, H, DV), S_final


def _gla_mixer(parts_x, parts_c, w_a2, b_a, g, ctx_out):
    f32 = jnp.float32
    w_a2, b_a, g = (t.astype(f32) for t in (w_a2, b_a, g))

    def prep(parts):
        q, k, v, r, lr = (t.astype(f32) for t in parts)
        B_, L, _ = q.shape
        q = q.reshape(B_, L, GLA_H, GLA_DK) * GLA_DK ** -0.5
        k = k.reshape(B_, L, GLA_H, GLA_DK)
        v = v.reshape(B_, L, GLA_H, GLA_DV)
        z = jnp.einsum('blzr,zrk->blzk', lr.reshape(B_, L, 2, GLA_RANK), w_a2) + b_a
        loga = (jax.nn.log_sigmoid(z) / GLA_TAU).reshape(B_, L, 2, GLA_H, GLA_DK)
        return [(q, k, v, loga[:, :, d]) for d in range(2)], r

    ins_x, r_x = prep(parts_x)
    ins_c, r_c = prep(parts_c)

    def run(inp, d, s0, want_out):
        return _gla_dir(*inp, s0, want_out)

    S0 = jnp.zeros((r_c.shape[0], GLA_H, GLA_DK, GLA_DV), f32)
    ox, oc = _bidir(run, ins_c, ins_x, S0, ctx_out)

    def post(o, r):
        B_, L = r.shape[:2]
        return jax.nn.silu(r) * (_ln(o) * g.reshape(GLA_H, GLA_DV)).reshape(B_, L, GLA_W)

    return post(ox, r_x), (post(oc, r_c) if ctx_out else None)


def _mlstm_dir(q, k, v, ig, lf, state0, want_out):
    B_, L, H, DK = q.shape
    DV = v.shape[-1]
    N = L // CHUNK
    q, k = (t.reshape(B_, N, CHUNK, H, DK) for t in (q, k))
    v = v.reshape(B_, N, CHUNK, H, DV)
    ig, lf = (t.reshape(B_, N, CHUNK, H) for t in (ig, lf))
    F = jnp.cumsum(lf, axis=2)
    F_last = F[:, :, -1]
    g = F_last[:, :, None] - F + ig
    m_loc = jnp.max(g, axis=2)
    w = jnp.exp(g - m_loc[:, :, None])
    dC = jnp.einsum('bnsh,bnshk,bnshv->bnhkv', w, k, v)
    dn = jnp.einsum('bnsh,bnshk->bnhk', w, k)

    def step(carry, inp):
        C, n, m = carry
        fl, ml, dc, dnn = inp
        m_new = jnp.maximum(fl + m, ml)
        a = jnp.exp(fl + m - m_new)
        bb = jnp.exp(ml - m_new)
        C_new = a[..., None, None] * C + bb[..., None, None] * dc
        n_new = a[..., None] * n + bb[..., None] * dnn
        return (C_new, n_new, m_new), (C, n, m)

    xs = tuple(jnp.moveaxis(t, 1, 0) for t in (F_last, m_loc, dC, dn))
    final, enter = lax.scan(step, state0, xs)
    if not want_out:
        return None, final
    C_e, n_e, m_e = (jnp.moveaxis(t, 0, 1) for t in enter)
    causal = jnp.tril(jnp.ones((CHUNK, CHUNK), dtype=bool))
    Dlog = F[:, :, :, None, :] - F[:, :, None, :, :] + ig[:, :, None, :, :]
    Dlog = jnp.where(causal[None, None, :, :, None], Dlog, -jnp.inf)
    inter = F + m_e[:, :, None]
    m_t = jnp.maximum(inter, jnp.max(Dlog, axis=3))
    w_inter = jnp.exp(inter - m_t)
    P = jnp.exp(Dlog - m_t[:, :, :, None]) * jnp.einsum('bnthk,bnshk->bntsh', q, k)
    num = w_inter[..., None] * jnp.einsum('bnthk,bnhkv->bnthv', q, C_e) + jnp.einsum('bntsh,bnshv->bnthv', P, v)
    den = w_inter * jnp.einsum('bnthk,bnhk->bnth', q, n_e) + jnp.sum(P, axis=3)
    h = num / jnp.maximum(jnp.abs(den), jnp.exp(-m_t))[..., None]
    return h.reshape(B_, L, H, DV), final


def _mlstm_mixer(parts_x, parts_c, i_bias, f_bias, g, ctx_out):
    f32 = jnp.float32
    i_bias, f_bias, g = (t.astype(f32) for t in (i_bias, f_bias, g))

    def prep(parts):
        q, k, v, o, ig, fg = (t.astype(f32) for t in parts)
        B_, L, _ = q.shape
        q = q.reshape(B_, L, ML_H, ML_D)
        k = k.reshape(B_, L, ML_H, ML_D) * ML_D ** -0.5
        v = v.reshape(B_, L, ML_H, ML_D)
        ig = ig.reshape(B_, L, 2, ML_H) + i_bias
        lf = jax.nn.log_sigmoid(fg.reshape(B_, L, 2, ML_H) + f_bias)
        return [(q, k, v, ig[:, :, d], lf[:, :, d]) for d in range(2)], o

    ins_x, o_x = prep(parts_x)
    ins_c, o_c = prep(parts_c)
    Bc = o_c.shape[0]
    state0 = (jnp.zeros((Bc, ML_H, ML_D, ML_D), f32), jnp.zeros((Bc, ML_H, ML_D), f32), jnp.zeros((Bc, ML_H), f32))

    def run(inp, d, s0, want_out):
        return _mlstm_dir(*inp, s0, want_out)

    hx, hc = _bidir(run, ins_c, ins_x, state0, ctx_out)

    def post(h, o):
        B_, L = o.shape[:2]
        return jax.nn.sigmoid(o) * (_ln(h) * g.reshape(ML_H, ML_D)).reshape(B_, L, ML_W)

    return post(hx, o_x), (post(hc, o_c) if ctx_out else None)


def _token_mixers(zx, zc, s5_a_re, s5_a_im, s5_log_dt, s5_b_re, s5_b_im, s5_c_re, s5_c_im, s5_d,
                  s5_w_glu, s5_b_glu, gla_w_a2, gla_b_a, gla_g, ml_i_bias, ml_f_bias, ml_g, ctx_out):
    px = jnp.split(zx, IN_OFFSETS, axis=-1)
    pc = jnp.split(zc, IN_OFFSETS, axis=-1)
    s5x, s5c = _s5_mixer(px[0], pc[0], s5_a_re, s5_a_im, s5_log_dt, s5_b_re, s5_b_im, s5_c_re, s5_c_im,
                         s5_d, s5_w_glu, s5_b_glu, ctx_out)
    glx, glc = _gla_mixer(px[1:6], pc[1:6], gla_w_a2, gla_b_a, gla_g, ctx_out)
    mlx, mlc = _mlstm_mixer(px[6:12], pc[6:12], ml_i_bias, ml_f_bias, ml_g, ctx_out)
    out_x = jnp.concatenate([s5x, glx, mlx], axis=-1).astype(zx.dtype)
    out_c = jnp.concatenate([s5c, glc, mlc], axis=-1).astype(zc.dtype) if ctx_out else None
    return out_x, out_c


def _conv_ffn(h, w_up, w_dconv, b_dconv, w_down, rows):
    B_, L, _ = h.shape
    a, v = jnp.split(h @ w_up, 2, axis=-1)
    a = a.reshape(B_, rows, L // rows, D_FF)
    a = lax.conv_general_dilated(a, w_dconv[:, :, None, :], (1, 1), 'SAME',
                                 dimension_numbers=('NHWC', 'HWIO', 'NHWC'), feature_group_count=D_FF)
    a = a.reshape(B_, L, D_FF) + b_dconv
    return (jax.nn.gelu(a) * v) @ w_down


def setup_inputs(seed: int = 0) -> dict:
    key = jax.random.key(seed)
    ks = iter(jax.random.split(key, 48))
    f32 = jnp.float32

    def nrm(shape, scale):
        return scale * jax.random.normal(next(ks), shape, f32)

    Lh, D = DEPTH, D_MODEL
    n_idx = jnp.arange(S5_P, dtype=f32)
    return {
        'x': nrm((BATCH, SEQ, D), 1.0),
        'c': nrm((BATCH, D), 1.0),
        'ctx': nrm((BATCH, CTX_LEN, D), 1.0),
        'c_ctx': nrm((D,), 1.0),
        'w_ada': nrm((Lh, D, 6 * D), 0.5 * D ** -0.5),
        'b_ada': nrm((Lh, 6 * D), 0.02),
        'w_in': nrm((Lh, D, D_IN), D ** -0.5),
        's5_a_re': -0.5 * jnp.exp(nrm((Lh, 2, S5_G, S5_P), 0.05)),
        's5_a_im': math.pi * n_idx + nrm((Lh, 2, S5_G, S5_P), 0.05),
        's5_log_dt': jax.random.uniform(next(ks), (Lh, 2, S5_G), f32, math.log(1e-3), math.log(1e-1)),
        's5_b_re': nrm((Lh, 2, S5_G, S5_P, S5_HC), (2.0 * S5_HC) ** -0.5),
        's5_b_im': nrm((Lh, 2, S5_G, S5_P, S5_HC), (2.0 * S5_HC) ** -0.5),
        's5_c_re': nrm((Lh, 2, S5_G, S5_HC, S5_P), S5_P ** -0.5),
        's5_c_im': nrm((Lh, 2, S5_G, S5_HC, S5_P), S5_P ** -0.5),
        's5_d': nrm((Lh, S5_W), 1.0),
        's5_w_glu': nrm((Lh, S5_W, S5_W), S5_W ** -0.5),
        's5_b_glu': nrm((Lh, S5_W), 0.02),
        'gla_w_a2': nrm((Lh, 2, GLA_RANK, GLA_H * GLA_DK), GLA_RANK ** -0.5),
        'gla_b_a': nrm((Lh, 2, GLA_H * GLA_DK), 0.1),
        'gla_g': 1.0 + nrm((Lh, GLA_W), 0.02),
        'ml_i_bias': nrm((Lh, 2, ML_H), 0.1),
        'ml_f_bias': jnp.linspace(3.0, 6.0, ML_H, dtype=f32) + nrm((Lh, 2, ML_H), 0.1),
        'ml_g': 1.0 + nrm((Lh, ML_W), 0.02),
        'w_out': nrm((Lh, D, D), BETA * D ** -0.5),
        'ln1_g': 1.0 + nrm((Lh, D), 0.02),
        'ln1_b': nrm((Lh, D), 0.02),
        'w_up': nrm((Lh, D, 2 * D_FF), D ** -0.5),
        'w_dconv': nrm((Lh, 3, 3, D_FF), 1.0 / 3.0),
        'b_dconv': nrm((Lh, D_FF), 0.02),
        'w_down': nrm((Lh, D_FF, D), BETA * D_FF ** -0.5),
        'ln2_g': 1.0 + nrm((Lh, D), 0.02),
        'ln2_b': nrm((Lh, D), 0.02),
    }


def reference(x, c, ctx, c_ctx, w_ada, b_ada, w_in, s5_a_re, s5_a_im, s5_log_dt, s5_b_re, s5_b_im,
              s5_c_re, s5_c_im, s5_d, s5_w_glu, s5_b_glu, gla_w_a2, gla_b_a, gla_g, ml_i_bias, ml_f_bias,
              ml_g, w_out, ln1_g, ln1_b, w_up, w_dconv, b_dconv, w_down, ln2_g, ln2_b):
    rows = x.shape[1] // GRID_W
    hx, hc = x, ctx
    for l in range(DEPTH):
        last = l == DEPTH - 1
        mod_x = jax.nn.silu(c) @ w_ada[l] + b_ada[l]
        mod_c = jax.nn.silu(c_ctx) @ w_ada[l] + b_ada[l]
        sh1, sc1, g1, sh2, sc2, g2 = (m[:, None, :] for m in jnp.split(mod_x, 6, axis=-1))
        csh1, csc1, cg1, csh2, csc2, cg2 = jnp.split(mod_c, 6, axis=-1)
        zx = _modulate(hx, sh1, sc1) @ w_in[l]
        zc = _modulate(hc, csh1, csc1) @ w_in[l]
        mx, mc = _token_mixers(zx, zc, s5_a_re[l], s5_a_im[l], s5_log_dt[l], s5_b_re[l], s5_b_im[l],
                               s5_c_re[l], s5_c_im[l], s5_d[l], s5_w_glu[l], s5_b_glu[l], gla_w_a2[l],
                               gla_b_a[l], gla_g[l], ml_i_bias[l], ml_f_bias[l], ml_g[l], not last)
        hx = _post_ln(ALPHA * hx + g1 * (mx @ w_out[l]), ln1_g[l], ln1_b[l])
        fx = _conv_ffn(_modulate(hx, sh2, sc2), w_up[l], w_dconv[l], b_dconv[l], w_down[l], rows)
        hx = _post_ln(ALPHA * hx + g2 * fx, ln2_g[l], ln2_b[l])
        if not last:
            hc = _post_ln(ALPHA * hc + cg1 * (mc @ w_out[l]), ln1_g[l], ln1_b[l])
            fc = _conv_ffn(_modulate(hc, csh2, csc2), w_up[l], w_dconv[l], b_dconv[l], w_down[l], 1)
            hc = _post_ln(ALPHA * hc + cg2 * fc, ln2_g[l], ln2_b[l])
    return hx
```

```python
import functools
import math

import numpy as np
import jax
import jax.numpy as jnp
from jax import lax
from jax.experimental import pallas as pl
from jax.experimental.pallas import tpu as pltpu

F32 = jnp.float32
BF16 = jnp.bfloat16
HIGHEST = lax.Precision.HIGHEST

D_MODEL = 1024
GRID_W = 64
CHUNK = 64
S5_W = D_MODEL // 4
S5_HC = 16
S5_G = S5_W // S5_HC
S5_P = 64
S5_T = 16
GLA_W = 3 * D_MODEL // 8
GLA_H = 4
GLA_DV = GLA_W // GLA_H
GLA_DK = GLA_DV // 2
GLA_RANK = 16
GLA_TAU = 16.0
ML_W = 3 * D_MODEL // 8
ML_H = 4
ML_D = ML_W // ML_H
D_FF = ((8 * D_MODEL // 3 + 127) // 128) * 128
LN_EPS = 1e-5

LANE = 128
HPAD = GLA_H * LANE
TB = 256
NCB = TB // CHUNK
Z_GQ, Z_GK, Z_GV, Z_GR = 0, HPAD, 2 * HPAD, 3 * HPAD
Z_MQ, Z_MK, Z_MV, Z_MO = 4 * HPAD, 5 * HPAD, 6 * HPAD, 7 * HPAD
Z_U = 8 * HPAD
NZM = Z_U + S5_W
NGATE = 4 * LANE
NGATE_T = 32
GATE_OFF = 16
MIX_W = S5_W + 2 * HPAD

VMEM_LIMIT = 56 * 1024 * 1024


def _cparams(sem):
    return pltpu.CompilerParams(dimension_semantics=sem, vmem_limit_bytes=VMEM_LIMIT)


def _ln_rows(x):
    mu = jnp.mean(x, axis=-1, keepdims=True)
    xc = x - mu
    var = jnp.mean(xc * xc, axis=-1, keepdims=True)
    return xc * lax.rsqrt(var + LN_EPS)


def _log_sigmoid(x):
    return jnp.minimum(x, 0.0) - jnp.log1p(jnp.exp(-jnp.abs(x)))


def _split_dot(tri_b, x, left):
    hi = x.astype(BF16)
    lo = (x - hi.astype(F32)).astype(BF16)
    if left:
        return (jnp.dot(tri_b, hi, preferred_element_type=F32)
                + jnp.dot(tri_b, lo, preferred_element_type=F32))
    return (jnp.dot(hi, tri_b, preferred_element_type=F32)
            + jnp.dot(lo, tri_b, preferred_element_type=F32))


_NT = (((1,), (1,)), ((), ()))
_TN = (((0,), (0,)), ((), ()))


def _mod_kernel(c_ref, w_ref, b_ref, o_ref):
    cc = c_ref[...]
    s = cc * jax.nn.sigmoid(cc)
    o_ref[0] = jnp.dot(s, w_ref[0], preferred_element_type=F32, precision=HIGHEST) + b_ref[0]


def _modulation(cc, w_ada, b_ada):
    depth, d, n = w_ada.shape
    nb = 4
    bn = n // nb
    return pl.pallas_call(
        _mod_kernel,
        grid=(depth, nb),
        in_specs=[pl.BlockSpec((8, d), lambda l, j: (0, 0)),
                  pl.BlockSpec((1, d, bn), lambda l, j: (l, 0, j)),
                  pl.BlockSpec((1, 1, bn), lambda l, j: (l, 0, j))],
        out_specs=pl.BlockSpec((1, 8, bn), lambda l, j: (l, 0, j)),
        out_shape=jax.ShapeDtypeStruct((depth, 8, n), F32),
        compiler_params=_cparams(("arbitrary", "arbitrary")),
    )(cc, w_ada, b_ada.reshape(depth, 1, n))


def _in_kernel(h_ref, mod_ref, w_ref, wg_ref, wgt_ref, z_ref, g_ref, gt_ref):
    x = h_ref[0]
    mod = mod_ref[0, 0]
    hm = (_ln_rows(x) * (1.0 + mod[1:2]) + mod[0:1]).astype(BF16)
    for c0 in range(0, NZM, HPAD):
        c1 = min(c0 + HPAD, NZM)
        z_ref[0, :, c0:c1] = jnp.dot(hm, w_ref[:, c0:c1], preferred_element_type=F32).astype(BF16)
    g_ref[0] = jnp.dot(hm, wg_ref[...], preferred_element_type=F32)
    gt_ref[0] = lax.dot_general(wgt_ref[...], hm, _NT, preferred_element_type=F32)


def _in_proj(h, modtab, w_p, w_g, w_gt):
    b, lt, d = h.shape
    nb = lt // TB
    const = lambda shape: pl.BlockSpec(shape, lambda bi, j: (0,) * len(shape))
    return pl.pallas_call(
        _in_kernel,
        grid=(b, nb),
        in_specs=[pl.BlockSpec((1, TB, d), lambda bi, j: (bi, j, 0)),
                  pl.BlockSpec((1, 1, 6, d), lambda bi, j: (bi, jnp.minimum(j, 1), 0, 0)),
                  const((d, NZM)), const((d, NGATE)), const((NGATE_T, d))],
        out_specs=[pl.BlockSpec((1, TB, NZM), lambda bi, j: (bi, j, 0)),
                   pl.BlockSpec((1, TB, NGATE), lambda bi, j: (bi, j, 0)),
                   pl.BlockSpec((1, NGATE_T, TB), lambda bi, j: (bi, 0, j))],
        out_shape=[jax.ShapeDtypeStruct((b, lt, NZM), BF16),
                   jax.ShapeDtypeStruct((b, lt, NGATE), F32),
                   jax.ShapeDtypeStruct((b, NGATE_T, lt), F32)],
        compiler_params=_cparams(("arbitrary", "arbitrary")),
    )(h, modtab, w_p, w_g, w_gt)


def _scan_block(d, i, nb):
    return jnp.where(d == 0, i, jnp.where(i == 0, 0, nb - i))


def _gla_kernel(q_ref, k_ref, v_ref, g_ref, wa_ref, ba_ref, tri_ref, o_ref, st_ref):
    d = pl.program_id(0)

    @pl.when(pl.program_id(2) == 0)
    def _():
        st_ref[...] = jnp.zeros_like(st_ref)

    tri = tri_ref[0]
    tri_b = tri.astype(BF16)
    causal = tri > 0.0
    wa = wa_ref[0].astype(BF16)
    ba = ba_ref[0]

    def chunk(j, carry):
        c = jnp.where(d == 0, j, NCB - 1 - j)
        r0 = pl.multiple_of(c * CHUNK, CHUNK)
        rows = pl.ds(r0, CHUNK)
        q = q_ref[0, rows, :].astype(F32) * (GLA_DK ** -0.5)
        k = k_ref[0, rows, :].astype(F32)
        v = v_ref[0, rows, :]
        g = g_ref[0, rows, :]
        zz = jnp.dot(g.astype(BF16), wa, preferred_element_type=F32) + ba
        loga = _log_sigmoid(zz) * (1.0 / GLA_TAU)
        bcum = _split_dot(tri_b, loga, True)
        btot = jnp.where(d == 0, bcum[CHUNK - 1:CHUNK], bcum[0:1])
        qd = (q * jnp.exp(bcum)).astype(BF16)
        kd = (k * jnp.exp(-bcum)).astype(BF16)
        kdec = (k * jnp.exp(btot - bcum)).astype(BF16)
        dec = jnp.exp(btot)
        for hh in range(GLA_H):
            sl = slice(hh * LANE, (hh + 1) * LANE)
            att = lax.dot_general(qd[:, sl], kd[:, sl], _NT, preferred_element_type=F32)
            att = jnp.where(causal, att, 0.0).astype(BF16)
            st = st_ref[hh]
            o = (lax.dot_general(qd[:, sl], st.astype(BF16), _NT, preferred_element_type=F32)
                 + jnp.dot(att, v[:, sl], preferred_element_type=F32))
            o_ref[0, 0, rows, sl] = o
            st_ref[hh] = st * dec[:, sl] + lax.dot_general(v[:, sl], kdec[:, sl], _TN,
                                                           preferred_element_type=F32)
        return carry

    lax.fori_loop(0, NCB, chunk, 0)


def _gla(z, gates, wa, ba, tri):
    b, lt, _ = z.shape
    nb = lt // TB
    zspec = lambda col: pl.BlockSpec((1, TB, HPAD), lambda d, bi, i: (bi, _scan_block(d, i, nb), col))
    dconst = lambda shape: pl.BlockSpec((1,) + shape, lambda d, bi, i: (d,) + (0,) * len(shape))
    return pl.pallas_call(
        _gla_kernel,
        grid=(2, b, nb),
        in_specs=[zspec(Z_GQ // HPAD), zspec(Z_GK // HPAD), zspec(Z_GV // HPAD),
                  pl.BlockSpec((1, TB, LANE), lambda d, bi, i: (bi, _scan_block(d, i, nb), 2 * d)),
                  dconst((LANE, HPAD)), dconst((1, HPAD)), dconst((CHUNK, CHUNK))],
        out_specs=pl.BlockSpec((1, 1, TB, HPAD), lambda d, bi, i: (d, bi, _scan_block(d, i, nb), 0)),
        out_shape=jax.ShapeDtypeStruct((2, b, lt, HPAD), F32),
        scratch_shapes=[pltpu.VMEM((GLA_H, LANE, LANE), F32)],
        compiler_params=_cparams(("arbitrary", "arbitrary", "arbitrary")),
    )(z, z, z, gates, wa, ba, tri)


def _mlstm_kernel(q_ref, k_ref, v_ref, ga_ref, gb_ref, gat_ref, gbt_ref, ibr_ref, fbr_ref,
                  ibc_ref, fbc_ref, tri_ref, trit_ref, o_ref, ct_ref, m_ref):
    d = pl.program_id(0)

    @pl.when(pl.program_id(2) == 0)
    def _():
        ct_ref[...] = jnp.zeros_like(ct_ref)
        m_ref[...] = jnp.zeros_like(m_ref)

    tri = tri_ref[0]
    tri_b = tri.astype(BF16)
    trit_b = trit_ref[0].astype(BF16)
    causal = tri > 0.0
    ones_col = lax.broadcasted_iota(jnp.int32, (CHUNK, LANE), 1) == ML_D

    def chunk(j, carry):
        c = jnp.where(d == 0, j, NCB - 1 - j)
        r0 = pl.multiple_of(c * CHUNK, CHUNK)
        rows = pl.ds(r0, CHUNK)
        q = q_ref[0, rows, :]
        k = k_ref[0, rows, :].astype(F32) * (ML_D ** -0.5)
        v = v_ref[0, rows, :]
        igc = ga_ref[0, rows, :] + ibr_ref[0]
        lfc = _log_sigmoid(gb_ref[0, rows, :] + fbr_ref[0])
        igr = gat_ref[0, c] + ibc_ref[0]
        lfr = _log_sigmoid(gbt_ref[0, c] + fbc_ref[0])
        fc = _split_dot(tri_b, lfc, True)
        fr = _split_dot(trit_b, lfr, False)
        ftot = jnp.where(d == 0, fc[CHUNK - 1:CHUNK], fc[0:1])
        gc = ftot - fc + igc
        m_loc = jnp.max(gc, axis=0, keepdims=True)
        wc = jnp.exp(gc - m_loc)
        m_old = m_ref[...]
        inter_all = fc + m_old
        m_new = jnp.maximum(ftot + m_old, m_loc)
        a_all = jnp.exp(ftot + m_old - m_new)
        b_all = jnp.exp(m_loc - m_new)
        for hh in range(ML_H):
            sl = slice(hh * LANE, (hh + 1) * LANE)
            gl = slice(GATE_OFF + hh, GATE_OFF + hh + 1)
            qh = q[:, sl]
            kh = k[:, sl]
            vaug = jnp.where(ones_col, 1.0, v[:, sl].astype(F32)).astype(BF16)
            qk = lax.dot_general(qh, kh.astype(BF16), _NT, preferred_element_type=F32)
            dlog = fc[:, gl] - fr[hh:hh + 1, :] + igr[hh:hh + 1, :]
            dlog = jnp.where(causal, dlog, -jnp.inf)
            inter = inter_all[:, gl]
            m_t = jnp.maximum(inter, jnp.max(dlog, axis=1, keepdims=True))
            w_inter = jnp.exp(inter - m_t)
            p = (jnp.exp(dlog - m_t) * qk).astype(BF16)
            ct = ct_ref[hh]
            nd = (w_inter * lax.dot_general(qh, ct.astype(BF16), _NT, preferred_element_type=F32)
                  + jnp.dot(p, vaug, preferred_element_type=F32))
            den = nd[:, ML_D:ML_D + 1]
            o_ref[0, 0, rows, sl] = nd / jnp.maximum(jnp.abs(den), jnp.exp(-m_t))
            kw = (kh * wc[:, gl]).astype(BF16)
            ct_ref[hh] = a_all[:, gl] * ct + b_all[:, gl] * lax.dot_general(
                vaug, kw, _TN, preferred_element_type=F32)
        m_ref[...] = m_new
        return carry

    lax.fori_loop(0, NCB, chunk, 0)


def _mlstm(z, gates, gates_t, ibr, fbr, ibc, fbc, tri, trit):
    b, lt, _ = z.shape
    nb = lt // TB
    zspec = lambda col: pl.BlockSpec((1, TB, HPAD), lambda d, bi, i: (bi, _scan_block(d, i, nb), col))
    gspec = lambda t: pl.BlockSpec((1, TB, LANE), lambda d, bi, i: (bi, _scan_block(d, i, nb), 2 * d + t))
    gates_t = jnp.transpose(gates_t.reshape(b, NGATE_T, lt // CHUNK, CHUNK), (0, 2, 1, 3))
    gtspec = lambda t: pl.BlockSpec(
        (1, NCB, 8, CHUNK), lambda d, bi, i: (bi, _scan_block(d, i, nb), 2 * d + t, 0))
    dconst = lambda shape: pl.BlockSpec((1,) + shape, lambda d, bi, i: (d,) + (0,) * len(shape))
    return pl.pallas_call(
        _mlstm_kernel,
        grid=(2, b, nb),
        in_specs=[zspec(Z_MQ // HPAD), zspec(Z_MK // HPAD), zspec(Z_MV // HPAD),
                  gspec(0), gspec(1), gtspec(0), gtspec(1),
                  dconst((1, LANE)), dconst((1, LANE)), dconst((8, 1)), dconst((8, 1)),
                  dconst((CHUNK, CHUNK)), dconst((CHUNK, CHUNK))],
        out_specs=pl.BlockSpec((1, 1, TB, HPAD), lambda d, bi, i: (d, bi, _scan_block(d, i, nb), 0)),
        out_shape=jax.ShapeDtypeStruct((2, b, lt, HPAD), F32),
        scratch_shapes=[pltpu.VMEM((ML_H, LANE, LANE), F32), pltpu.VMEM((1, LANE), F32)],
        compiler_params=_cparams(("arbitrary", "arbitrary", "arbitrary")),
    )(z, z, z, gates, gates, gates_t, gates_t, ibr, fbr, ibc, fbc, tri, trit)


def _s5_prep_kernel(arc_ref, aic_ref, arr_ref, air_ref, ldt_ref, br_ref, bi_ref, cr_ref, ci_ref,
                    crt_ref, cit_ref, ar_out, ai_out, k_out, k0_out, vr_out, vi_out, l_out):
    nj = S5_T * S5_HC
    jmap = (lax.broadcasted_iota(jnp.int32, (1, nj), 1) // S5_HC).astype(F32)
    fmap = (lax.broadcasted_iota(jnp.int32, (nj, 1), 0) // S5_HC + 1).astype(F32)
    kmats = []
    for d in range(2):
        dt = jnp.exp(ldt_ref[0, d])
        a_re, a_im = arc_ref[0, d], aic_ref[0, d]
        arc, aic = a_re * dt, a_im * dt
        mag = jnp.exp(arc)
        lre, lim = mag * jnp.cos(aic), mag * jnp.sin(aic)
        den = a_re * a_re + a_im * a_im
        z_re = ((lre - 1.0) * a_re + lim * a_im) / den
        z_im = (lim * a_re - (lre - 1.0) * a_im) / den
        b_re, b_im = br_ref[0, d], bi_ref[0, d]
        bb_re = z_re * b_re - z_im * b_im
        bb_im = z_re * b_im + z_im * b_re
        pmag = jnp.exp(arc * jmap)
        p_re, p_im = pmag * jnp.cos(aic * jmap), pmag * jnp.sin(aic * jmap)
        a_st_re = p_re * bb_re - p_im * bb_im
        a_st_im = p_re * bb_im + p_im * bb_re
        ar_out[0, d] = a_st_re
        ai_out[0, d] = a_st_im
        kmat = (jnp.dot(cr_ref[0, d], a_st_re, preferred_element_type=F32, precision=HIGHEST)
                - jnp.dot(ci_ref[0, d], a_st_im, preferred_element_type=F32, precision=HIGHEST))
        k_out[0, d] = kmat
        kmats.append(kmat)
        arr, air = arr_ref[0, d] * dt, air_ref[0, d] * dt
        fmag = jnp.exp(arr * fmap)
        f_re, f_im = fmag * jnp.cos(air * fmap), fmag * jnp.sin(air * fmap)
        c_re, c_im = crt_ref[0, d], cit_ref[0, d]
        vr_out[0, d] = c_re * f_re - c_im * f_im
        vi_out[0, d] = -(c_re * f_im + c_im * f_re)
        tmag = jnp.exp(arr * float(S5_T))
        l_out[0, d, 0:1, :] = tmag * jnp.cos(air * float(S5_T))
        l_out[0, d, 1:2, :] = tmag * jnp.sin(air * float(S5_T))
    k0_out[0] = kmats[0][:, 0:S5_HC] + kmats[1][:, 0:S5_HC]


def _s5_operators(a_re, a_im, log_dt, b_re, b_im, c_re, c_im):
    g, p, hc, t = S5_G, S5_P, S5_HC, S5_T
    nj = t * hc
    gm = lambda x: jnp.moveaxis(x, 0, 1)
    spec = lambda shape: pl.BlockSpec((1,) + shape, lambda i: (i,) + (0,) * len(shape))
    outs = pl.pallas_call(
        _s5_prep_kernel,
        grid=(g,),
        in_specs=[spec((2, p, 1)), spec((2, p, 1)), spec((2, 1, p)), spec((2, 1, p)), spec((2, 1, 1)),
                  spec((2, p, nj)), spec((2, p, nj)), spec((2, hc, p)), spec((2, hc, p)),
                  spec((2, nj, p)), spec((2, nj, p))],
        out_specs=[spec((2, p, nj)), spec((2, p, nj)), spec((2, hc, nj)), spec((hc, hc)),
                   spec((2, nj, p)), spec((2, nj, p)), spec((2, 2, p))],
        out_shape=[jax.ShapeDtypeStruct((g, 2, p, nj), F32), jax.ShapeDtypeStruct((g, 2, p, nj), F32),
                   jax.ShapeDtypeStruct((g, 2, hc, nj), F32), jax.ShapeDtypeStruct((g, hc, hc), F32),
                   jax.ShapeDtypeStruct((g, 2, nj, p), F32), jax.ShapeDtypeStruct((g, 2, nj, p), F32),
                   jax.ShapeDtypeStruct((g, 2, 2, p), F32)],
        compiler_params=_cparams(("arbitrary",)),
    )(gm(a_re)[..., None], gm(a_im)[..., None], gm(a_re)[:, :, None, :], gm(a_im)[:, :, None, :],
      gm(log_dt)[..., None, None],
      jnp.tile(gm(b_re), (1, 1, 1, t)), jnp.tile(gm(b_im), (1, 1, 1, t)), gm(c_re), gm(c_im),
      jnp.tile(gm(c_re), (1, 1, t, 1)), jnp.tile(gm(c_im), (1, 1, t, 1)))
    a_st_re, a_st_im, kmat, k0, v_re, v_im, lam_t = outs

    a5 = jnp.stack([a_st_re, a_st_im], axis=2).reshape(g, 2, 2, p, t, hc)
    a5 = jnp.stack([jnp.flip(a5[:, 0], axis=3), a5[:, 1]], axis=1)
    wcat = jnp.transpose(a5, (0, 4, 5, 1, 2, 3)).reshape(g, nj, 4 * p)
    v5 = jnp.stack([v_re, v_im], axis=2).reshape(g, 2, 2, t, hc, p)
    v5 = jnp.stack([v5[:, 0], jnp.flip(v5[:, 1], axis=2)], axis=1)
    vcat = jnp.transpose(v5, (0, 1, 2, 5, 3, 4)).reshape(g, 4 * p, nj)
    k5 = kmat.reshape(g, 2, hc, t, hc)
    ss = np.arange(t)[:, None]
    tt = np.arange(t)[None, :]
    lag = np.abs(tt - ss)
    kf = k5[:, 0][:, :, lag, :]
    kb = k5[:, 1][:, :, lag, :]
    sel_f = jnp.asarray(tt > ss)[None, None, :, :, None]
    sel_b = jnp.asarray(ss > tt)[None, None, :, :, None]
    m5 = jnp.where(sel_f, kf, jnp.where(sel_b, kb, k0[:, :, None, None, :]))
    mfb = jnp.transpose(m5, (0, 2, 4, 3, 1)).reshape(g, nj, nj)
    return wcat.astype(BF16), vcat.astype(BF16), mfb.astype(BF16), lam_t


def _s5_dx_kernel(u_ref, w_ref, o_ref):
    o_ref[0] = jnp.dot(u_ref[0], w_ref[0], preferred_element_type=F32)


def _s5_scan_kernel(dxr_ref, dxi_ref, lr_ref, li_ref, xr_out, xi_out, sr_ref, si_ref):
    @pl.when(pl.program_id(0) == 0)
    def _():
        sr_ref[...] = jnp.zeros_like(sr_ref)
        si_ref[...] = jnp.zeros_like(si_ref)

    lr, li = lr_ref[...], li_ref[...]

    def step(n, carry):
        xr, xi = carry
        xr_out[n] = xr
        xi_out[n] = xi
        return (lr * xr - li * xi + dxr_ref[n], lr * xi + li * xr + dxi_ref[n])

    xr, xi = lax.fori_loop(0, dxr_ref.shape[0], step, (sr_ref[...], si_ref[...]))
    sr_ref[...] = xr
    si_ref[...] = xi


def _s5_y_kernel(u_ref, xe_ref, m_ref, v_ref, o_ref):
    o_ref[0] = (jnp.dot(u_ref[0], m_ref[0], preferred_element_type=F32)
                + jnp.dot(xe_ref[0], v_ref[0], preferred_element_type=F32))


def _s5(u, wcat, vcat, mfb, lam_t, ctx_len):
    b, lt, _ = u.shape
    g, p, hc, t = S5_G, S5_P, S5_HC, S5_T
    nj = t * hc
    nc = lt // t
    ncx = ctx_len // t
    rows = b * nc
    ug = jnp.transpose(u.reshape(b, nc, t, g, hc), (3, 0, 1, 2, 4)).reshape(g, rows, nj)
    gspec = lambda shape: pl.BlockSpec((1,) + shape, lambda i: (i,) + (0,) * len(shape))
    dx = pl.pallas_call(
        _s5_dx_kernel,
        grid=(g,),
        in_specs=[gspec((rows, nj)), gspec((nj, 4 * p))],
        out_specs=gspec((rows, 4 * p)),
        out_shape=jax.ShapeDtypeStruct((g, rows, 4 * p), F32),
        compiler_params=_cparams(("arbitrary",)),
    )(ug, wcat)
    perm_b = np.concatenate([np.arange(ncx - 1, -1, -1), np.arange(nc - 1, ncx - 1, -1)])
    inv_b = np.argsort(perm_b)
    dx6 = jnp.transpose(dx.reshape(g, b, nc, 2, 2, p), (2, 3, 1, 4, 0, 5))
    dx6 = jnp.stack([dx6[:, 0], dx6[perm_b, 1]], axis=1)
    dxr = dx6[:, :, :, 0].reshape(nc, 2 * b, g * p)
    dxi = dx6[:, :, :, 1].reshape(nc, 2 * b, g * p)
    lam = jnp.transpose(lam_t, (1, 2, 0, 3)).reshape(2, 2, g * p)
    lr = jnp.repeat(lam[:, 0], b, axis=0)
    li = jnp.repeat(lam[:, 1], b, axis=0)
    nblk = 8 if nc % 8 == 0 else 1
    rb = nc // nblk
    sspec = pl.BlockSpec((rb, 2 * b, g * p), lambda i: (i, 0, 0))
    cspec = pl.BlockSpec((2 * b, g * p), lambda i: (0, 0))
    xer, xei = pl.pallas_call(
        _s5_scan_kernel,
        grid=(nblk,),
        in_specs=[sspec, sspec, cspec, cspec],
        out_specs=[sspec, sspec],
        out_shape=[jax.ShapeDtypeStruct((nc, 2 * b, g * p), F32)] * 2,
        scratch_shapes=[pltpu.VMEM((2 * b, g * p), F32)] * 2,
        compiler_params=_cparams(("arbitrary",)),
    )(dxr, dxi, lr, li)
    xe = jnp.stack([xer.reshape(nc, 2, b, g, p), xei.reshape(nc, 2, b, g, p)], axis=3)
    xe = jnp.stack([xe[:, 0], xe[inv_b, 1]], axis=1)
    xe = jnp.transpose(xe, (4, 2, 0, 1, 3, 5)).reshape(g, rows, 4 * p).astype(BF16)
    y = pl.pallas_call(
        _s5_y_kernel,
        grid=(g,),
        in_specs=[gspec((rows, nj)), gspec((rows, 4 * p)), gspec((nj, nj)), gspec((4 * p, nj))],
        out_specs=gspec((rows, nj)),
        out_shape=jax.ShapeDtypeStruct((g, rows, nj), F32),
        compiler_params=_cparams(("arbitrary",)),
    )(ug, xe, mfb, vcat)
    return jnp.transpose(y.reshape(g, b, nc, t, hc), (1, 2, 3, 0, 4)).reshape(b, lt, g * hc)


def _head_ln(o, gain):
    valid = lax.broadcasted_iota(jnp.int32, (1, LANE), 1) < GLA_DV
    outs = []
    for hh in range(GLA_H):
        oh = o[:, hh * LANE:(hh + 1) * LANE]
        mu = jnp.sum(jnp.where(valid, oh, 0.0), axis=-1, keepdims=True) * (1.0 / GLA_DV)
        oc = jnp.where(valid, oh - mu, 0.0)
        var = jnp.sum(oc * oc, axis=-1, keepdims=True) * (1.0 / GLA_DV)
        outs.append(oc * lax.rsqrt(var + LN_EPS))
    return jnp.concatenate(outs, axis=-1) * gain


def _out_kernel(alpha, h_ref, mod_ref, ys_ref, u_ref, gr_ref, mo_ref, gf_ref, gb_ref, mf_ref, mb_ref,
                dsk_ref, wglu_ref, bglu_ref, gg_ref, mg_ref, wout_ref, l1g_ref, l1b_ref, wup_ref,
                h1_ref, a_ref, v_ref, mix_ref):
    mod = mod_ref[0, 0]
    y1 = jax.nn.gelu(ys_ref[0] + dsk_ref[...] * u_ref[0].astype(F32))
    glu = jnp.dot(y1.astype(BF16), wglu_ref[...], preferred_element_type=F32) + bglu_ref[...]
    mix_ref[:, 0:S5_W] = (y1 * jax.nn.sigmoid(glu)).astype(BF16)
    r = gr_ref[0].astype(F32)
    glo = _head_ln(gf_ref[0, 0] + gb_ref[0, 0], gg_ref[...])
    mix_ref[:, S5_W:S5_W + HPAD] = (r * jax.nn.sigmoid(r) * glo).astype(BF16)
    mlo = _head_ln(mf_ref[0, 0] + mb_ref[0, 0], mg_ref[...])
    mix_ref[:, S5_W + HPAD:MIX_W] = (jax.nn.sigmoid(mo_ref[0].astype(F32)) * mlo).astype(BF16)
    mixed = jnp.dot(mix_ref[...], wout_ref[...], preferred_element_type=F32)
    y = alpha * h_ref[0] + mod[2:3] * mixed
    h1 = _ln_rows(y) * l1g_ref[...] + l1b_ref[...]
    h1_ref[0] = h1
    hm = (_ln_rows(h1) * (1.0 + mod[4:5]) + mod[3:4]).astype(BF16)
    a_ref[0] = jnp.dot(hm, wup_ref[:, 0:D_FF], preferred_element_type=F32).astype(BF16)
    v_ref[0] = jnp.dot(hm, wup_ref[:, D_FF:2 * D_FF], preferred_element_type=F32).astype(BF16)


def _out_proj(alpha, h, modtab, ys5, z, ogla, oml, dsk, wglu, bglu, gg, mg, wout, l1g, l1b, wup):
    b, lt, d = h.shape
    nb = lt // TB
    const = lambda shape: pl.BlockSpec(shape, lambda bi, j: (0,) * len(shape))
    tok = lambda w, col=0: pl.BlockSpec((1, TB, w), lambda bi, j: (bi, j, col))
    dirspec = lambda dd: pl.BlockSpec((1, 1, TB, HPAD), lambda bi, j: (dd, bi, j, 0))
    return pl.pallas_call(
        functools.partial(_out_kernel, alpha),
        grid=(b, nb),
        in_specs=[tok(d), pl.BlockSpec((1, 1, 6, d), lambda bi, j: (bi, jnp.minimum(j, 1), 0, 0)),
                  tok(S5_W), tok(S5_W, Z_U // S5_W), tok(HPAD, Z_GR // HPAD), tok(HPAD, Z_MO // HPAD),
                  dirspec(0), dirspec(1), dirspec(0), dirspec(1),
                  const((1, S5_W)), const((S5_W, S5_W)), const((1, S5_W)), const((1, HPAD)), const((1, HPAD)),
                  const((MIX_W, d)), const((1, d)), const((1, d)), const((d, 2 * D_FF))],
        out_specs=[tok(d), tok(D_FF), tok(D_FF)],
        out_shape=[jax.ShapeDtypeStruct((b, lt, d), F32), jax.ShapeDtypeStruct((b, lt, D_FF), BF16),
                   jax.ShapeDtypeStruct((b, lt, D_FF), BF16)],
        scratch_shapes=[pltpu.VMEM((TB, MIX_W), BF16)],
        compiler_params=_cparams(("arbitrary", "arbitrary")),
    )(h, modtab, ys5, z, z, z, ogla, ogla, oml, oml, dsk, wglu, bglu, gg, mg, wout, l1g, l1b, wup)


FF_CW = 256


def _ffn_kernel(alpha, nb, h_ref, mod_ref, a_ref, ap_ref, an_ref, v_ref, wc_ref, bc_ref, wd_ref,
                l2g_ref, l2b_ref, o_ref, g_ref):
    j = pl.program_id(1)
    mod = mod_ref[0, 0]
    is_lat = j > 0
    t_idx = lax.broadcasted_iota(jnp.int32, (TB, 1), 0)
    col = jnp.where(is_lat, t_idx % GRID_W, t_idx)
    last = jnp.where(is_lat, GRID_W - 1, TB - 1)
    has_l = (col > 0).astype(F32)
    has_r = (col < last).astype(F32)
    vert = is_lat.astype(F32)
    up_ok = (j > 1).astype(F32) * vert
    dn_ok = (j < nb - 1).astype(F32) * vert
    for c0 in range(0, D_FF, FF_CW):
        cs = slice(c0, c0 + FF_CW)
        a = a_ref[0, :, cs].astype(F32)
        prev = ap_ref[0, :, cs].astype(F32) * up_ok
        nxt = an_ref[0, :, cs].astype(F32) * dn_ok
        up = jnp.concatenate([prev, a[:TB - GRID_W]], axis=0) * vert
        dn = jnp.concatenate([a[GRID_W:], nxt], axis=0) * vert
        w = wc_ref[:, cs]
        acc = bc_ref[:, cs] + a * w[4:5]
        acc = acc + up * w[1:2] + dn * w[7:8]
        left = (pltpu.roll(up, 1, 0) * w[0:1] + pltpu.roll(a, 1, 0) * w[3:4]
                + pltpu.roll(dn, 1, 0) * w[6:7])
        right = (pltpu.roll(up, TB - 1, 0) * w[2:3] + pltpu.roll(a, TB - 1, 0) * w[5:6]
                 + pltpu.roll(dn, TB - 1, 0) * w[8:9])
        acc = acc + has_l * left + has_r * right
        g_ref[:, cs] = (jax.nn.gelu(acc) * v_ref[0, :, cs].astype(F32)).astype(BF16)
    f = jnp.dot(g_ref[...], wd_ref[...], preferred_element_type=F32)
    y = alpha * h_ref[0] + mod[5:6] * f
    o_ref[0] = _ln_rows(y) * l2g_ref[...] + l2b_ref[...]


def _ffn(alpha, h1, modtab, a, v, wc, bc, wd, l2g, l2b):
    b, lt, d = h1.shape
    nb = lt // TB
    rpb = TB // GRID_W
    nrow = lt // GRID_W
    const = lambda shape: pl.BlockSpec(shape, lambda bi, j: (0,) * len(shape))
    tok = lambda w: pl.BlockSpec((1, TB, w), lambda bi, j: (bi, j, 0))
    return pl.pallas_call(
        functools.partial(_ffn_kernel, alpha, nb),
        grid=(b, nb),
        in_specs=[tok(d), pl.BlockSpec((1, 1, 6, d), lambda bi, j: (bi, jnp.minimum(j, 1), 0, 0)),
                  tok(D_FF),
                  pl.BlockSpec((1, GRID_W, D_FF), lambda bi, j: (bi, jnp.maximum(j * rpb - 1, 0), 0)),
                  pl.BlockSpec((1, GRID_W, D_FF), lambda bi, j: (bi, jnp.minimum((j + 1) * rpb, nrow - 1), 0)),
                  tok(D_FF), const((9, D_FF)), const((1, D_FF)), const((D_FF, d)),
                  const((1, d)), const((1, d))],
        out_specs=tok(d),
        out_shape=jax.ShapeDtypeStruct((b, lt, d), F32),
        scratch_shapes=[pltpu.VMEM((TB, D_FF), BF16)],
        compiler_params=_cparams(("arbitrary", "arbitrary")),
    )(h1, modtab, a, a, a, v, wc, bc, wd, l2g, l2b)


def _pad_heads(w, nh, dh):
    lead = w.shape[:-1]
    w = w.reshape(lead + (nh, dh))
    w = jnp.pad(w, [(0, 0)] * len(lead) + [(0, 0), (0, LANE - dh)])
    return w.reshape(lead + (nh * LANE,))


def _pad_head_rows(w, nh, dh):
    return jnp.moveaxis(_pad_heads(jnp.moveaxis(w, 0, -1), nh, dh), -1, 0)


def _layer_weights(w_in, gla_w_a2, gla_b_a, ml_i_bias, ml_f_bias, w_out):
    d = w_in.shape[0]
    offs = np.cumsum((0, S5_W, GLA_H * GLA_DK, GLA_H * GLA_DK, GLA_W, GLA_W, 2 * GLA_RANK,
                      ML_W, ML_W, ML_W, ML_W, 2 * ML_H, 2 * ML_H))
    part = lambda i: w_in[:, offs[i]:offs[i + 1]]
    w_p = jnp.concatenate([
        _pad_heads(part(1), GLA_H, GLA_DK), _pad_heads(part(2), GLA_H, GLA_DK),
        _pad_heads(part(3), GLA_H, GLA_DV), _pad_heads(part(4), GLA_H, GLA_DV),
        _pad_heads(part(6), ML_H, ML_D), _pad_heads(part(7), ML_H, ML_D),
        _pad_heads(part(8), ML_H, ML_D), _pad_heads(part(9), ML_H, ML_D),
        part(0)], axis=1).astype(BF16)
    lr, ig, fg = part(5), part(10), part(11)
    zeros = lambda n: jnp.zeros((d, n), w_in.dtype)
    tiles = []
    for dd in range(2):
        tiles += [lr[:, dd * GLA_RANK:(dd + 1) * GLA_RANK], ig[:, dd * ML_H:(dd + 1) * ML_H],
                  zeros(LANE - GLA_RANK - ML_H),
                  zeros(GATE_OFF), fg[:, dd * ML_H:(dd + 1) * ML_H], zeros(LANE - GATE_OFF - ML_H)]
    w_g = jnp.concatenate(tiles, axis=1).astype(BF16)
    rows_t = []
    for dd in range(2):
        for gate in (ig, fg):
            rows_t += [gate[:, dd * ML_H:(dd + 1) * ML_H], zeros(8 - ML_H)]
    w_gt = jnp.concatenate(rows_t, axis=1).T.astype(BF16)
    wa = jnp.pad(_pad_heads(gla_w_a2, GLA_H, GLA_DK), ((0, 0), (0, LANE - GLA_RANK), (0, 0)))
    ba = _pad_heads(gla_b_a, GLA_H, GLA_DK)[:, None, :]
    gate_row = lambda bias: jnp.pad(bias, ((0, 0), (GATE_OFF, LANE - GATE_OFF - ML_H)))[:, None, :]
    gate_col = lambda bias: jnp.pad(bias, ((0, 0), (0, 8 - ML_H)))[:, :, None]
    wout_p = jnp.concatenate([
        w_out[:S5_W], _pad_head_rows(w_out[S5_W:S5_W + GLA_W], GLA_H, GLA_DV),
        _pad_head_rows(w_out[S5_W + GLA_W:], ML_H, ML_D)], axis=0).astype(BF16)
    return dict(w_p=w_p, w_g=w_g, w_gt=w_gt, wa=wa, ba=ba,
                ibr=gate_row(ml_i_bias), fbr=gate_row(ml_f_bias),
                ibc=gate_col(ml_i_bias), fbc=gate_col(ml_f_bias), wout=wout_p)


def kernel(x, c, ctx, c_ctx, w_ada, b_ada, w_in, s5_a_re, s5_a_im, s5_log_dt, s5_b_re, s5_b_im, s5_c_re, s5_c_im, s5_d, s5_w_glu, s5_b_glu, gla_w_a2, gla_b_a, gla_g, ml_i_bias, ml_f_bias, ml_g, w_out, ln1_g, ln1_b, w_up, w_dconv, b_dconv, w_down, ln2_g, ln2_b):
    b, seq, d = x.shape
    ctx_len = ctx.shape[1]
    depth = w_in.shape[0]
    assert d == D_MODEL and ctx_len == TB and seq % TB == 0 and b <= 7
    alpha = (2.0 * depth) ** 0.25

    cc = jnp.zeros((8, d), F32).at[:b].set(c).at[b].set(c_ctx)
    mod = _modulation(cc, w_ada, b_ada).reshape(depth, 8, 6, d)
    modtab = jnp.stack([jnp.broadcast_to(mod[:, b:b + 1], (depth, b, 6, d)), mod[:, :b]], axis=2)

    low = np.tril(np.ones((CHUNK, CHUNK), np.float32))
    tri = jnp.asarray(np.stack([low, low.T]))
    trit = jnp.asarray(np.stack([low.T, low]))

    h = jnp.concatenate([ctx, x], axis=1)
    for l in range(depth):
        lw = _layer_weights(w_in[l], gla_w_a2[l], gla_b_a[l], ml_i_bias[l], ml_f_bias[l], w_out[l])
        z, gates, gates_t = _in_proj(h, modtab[l], lw["w_p"], lw["w_g"], lw["w_gt"])
        ogla = _gla(z, gates, lw["wa"], lw["ba"], tri)
        oml = _mlstm(z, gates, gates_t, lw["ibr"], lw["fbr"], lw["ibc"], lw["fbc"], tri, trit)
        wcat, vcat, mfb, lam_t = _s5_operators(s5_a_re[l], s5_a_im[l], s5_log_dt[l], s5_b_re[l],
                                               s5_b_im[l], s5_c_re[l], s5_c_im[l])
        ys5 = _s5(z[:, :, Z_U:], wcat, vcat, mfb, lam_t, ctx_len)
        h1, a, v = _out_proj(alpha, h, modtab[l], ys5, z, ogla, oml, s5_d[l][None], s5_w_glu[l].astype(BF16),
                             s5_b_glu[l][None], _pad_heads(gla_g[l], GLA_H, GLA_DV)[None],
                             _pad_heads(ml_g[l], ML_H, ML_D)[None], lw["wout"], ln1_g[l][None],
                             ln1_b[l][None], w_up[l].astype(BF16))
        h = _ffn(alpha, h1, modtab[l], a, v, w_dconv[l].reshape(9, D_FF), b_dconv[l][None],
                 w_down[l].astype(BF16), ln2_g[l][None], ln2_b[l][None])
    return h[:, ctx_len:]
```

```python
import functools
import math

import numpy as np
import jax
import jax.numpy as jnp
from jax import lax
from jax.experimental import pallas as pl
from jax.experimental.pallas import tpu as pltpu

F32 = jnp.float32
BF16 = jnp.bfloat16
HIGHEST = lax.Precision.HIGHEST

D_MODEL = 1024
GRID_W = 64
CHUNK = 64
S5_W = D_MODEL // 4
S5_HC = 16
S5_G = S5_W // S5_HC
S5_P = 64
S5_T = 16
GLA_W = 3 * D_MODEL // 8
GLA_H = 4
GLA_DV = GLA_W // GLA_H
GLA_DK = GLA_DV // 2
GLA_RANK = 16
GLA_TAU = 16.0
ML_W = 3 * D_MODEL // 8
ML_H = 4
ML_D = ML_W // ML_H
D_FF = ((8 * D_MODEL // 3 + 127) // 128) * 128
LN_EPS = 1e-5

LANE = 128
HPAD = GLA_H * LANE
TB = 256
NCB = TB // CHUNK
Z_GQ, Z_GK, Z_GV, Z_GR = 0, HPAD, 2 * HPAD, 3 * HPAD
Z_MQ, Z_MK, Z_MV, Z_MO = 4 * HPAD, 5 * HPAD, 6 * HPAD, 7 * HPAD
NZM = 8 * HPAD
NGATE = 4 * LANE
NGATE_T = 32
GATE_OFF = 16
MIX_W = S5_W + 2 * HPAD

VMEM_LIMIT = 56 * 1024 * 1024


def _cparams(sem):
    return pltpu.CompilerParams(dimension_semantics=sem, vmem_limit_bytes=VMEM_LIMIT)


def _ln_rows(x):
    mu = jnp.mean(x, axis=-1, keepdims=True)
    xc = x - mu
    var = jnp.mean(xc * xc, axis=-1, keepdims=True)
    return xc * lax.rsqrt(var + LN_EPS)


def _log_sigmoid(x):
    return jnp.minimum(x, 0.0) - jnp.log1p(jnp.exp(-jnp.abs(x)))


def _split_dot(tri_b, x, left):
    hi = x.astype(BF16)
    lo = (x - hi.astype(F32)).astype(BF16)
    if left:
        return (jnp.dot(tri_b, hi, preferred_element_type=F32)
                + jnp.dot(tri_b, lo, preferred_element_type=F32))
    return (jnp.dot(hi, tri_b, preferred_element_type=F32)
            + jnp.dot(lo, tri_b, preferred_element_type=F32))


_NT = (((1,), (1,)), ((), ()))
_TN = (((0,), (0,)), ((), ()))


def _mod_kernel(c_ref, w_ref, b_ref, o_ref):
    cc = c_ref[...]
    s = cc * jax.nn.sigmoid(cc)
    o_ref[0] = jnp.dot(s, w_ref[0], preferred_element_type=F32, precision=HIGHEST) + b_ref[0]


def _modulation(cc, w_ada, b_ada):
    depth, d, n = w_ada.shape
    nb = 4
    bn = n // nb
    return pl.pallas_call(
        _mod_kernel,
        grid=(depth, nb),
        in_specs=[pl.BlockSpec((8, d), lambda l, j: (0, 0)),
                  pl.BlockSpec((1, d, bn), lambda l, j: (l, 0, j)),
                  pl.BlockSpec((1, 1, bn), lambda l, j: (l, 0, j))],
        out_specs=pl.BlockSpec((1, 8, bn), lambda l, j: (l, 0, j)),
        out_shape=jax.ShapeDtypeStruct((depth, 8, n), F32),
        compiler_params=_cparams(("arbitrary", "arbitrary")),
    )(cc, w_ada, b_ada.reshape(depth, 1, n))


def _in_kernel(h_ref, mod_ref, w_ref, wu_ref, wg_ref, wgt_ref, z_ref, ua_ref, ub_ref, g_ref, gt_ref):
    x = h_ref[0]
    mod = mod_ref[0, 0]
    hm = (_ln_rows(x) * (1.0 + mod[1:2]) + mod[0:1]).astype(BF16)
    for c0 in range(0, NZM, HPAD):
        z_ref[0, :, c0:c0 + HPAD] = jnp.dot(hm, w_ref[:, c0:c0 + HPAD],
                                            preferred_element_type=F32).astype(BF16)
    u = jnp.dot(hm, wu_ref[...], preferred_element_type=F32)
    ua_ref[0] = u[:, :LANE]
    ub_ref[0] = u[:, LANE:]
    g_ref[0] = jnp.dot(hm, wg_ref[...], preferred_element_type=F32)
    gt_ref[0] = lax.dot_general(wgt_ref[...], hm, _NT, preferred_element_type=F32)


def _in_proj(h, modtab, w_p, w_u, w_g, w_gt):
    b, lt, d = h.shape
    nb = lt // TB
    const = lambda shape: pl.BlockSpec(shape, lambda bi, j: (0,) * len(shape))
    tok = lambda w: pl.BlockSpec((1, TB, w), lambda bi, j: (bi, j, 0))
    return pl.pallas_call(
        _in_kernel,
        grid=(b, nb),
        in_specs=[tok(d),
                  pl.BlockSpec((1, 1, 6, d), lambda bi, j: (bi, jnp.minimum(j, 1), 0, 0)),
                  const((d, NZM)), const((d, S5_W)), const((d, NGATE)), const((NGATE_T, d))],
        out_specs=[tok(NZM), tok(LANE), tok(LANE), tok(NGATE),
                   pl.BlockSpec((1, NGATE_T, TB), lambda bi, j: (bi, 0, j))],
        out_shape=[jax.ShapeDtypeStruct((b, lt, NZM), BF16),
                   jax.ShapeDtypeStruct((b, lt, LANE), F32),
                   jax.ShapeDtypeStruct((b, lt, LANE), F32),
                   jax.ShapeDtypeStruct((b, lt, NGATE), F32),
                   jax.ShapeDtypeStruct((b, NGATE_T, lt), F32)],
        compiler_params=_cparams(("arbitrary", "arbitrary")),
    )(h, modtab, w_p, w_u, w_g, w_gt)


def _scan_block(d, i, nb):
    return jnp.where(d == 0, i, jnp.where(i == 0, 0, nb - i))


def _chunk_order(rev):
    return range(NCB - 1, -1, -1) if rev else range(NCB)


def _gla_kernel(rev, q_ref, k_ref, v_ref, g_ref, wa_ref, ba_ref, tri_ref, ones_ref, o_ref, st_ref):
    @pl.when(pl.program_id(1) == 0)
    def _():
        st_ref[...] = jnp.zeros_like(st_ref)

    tri = tri_ref[...]
    tri_b = tri.astype(BF16)
    ones_b = ones_ref[...].astype(BF16)
    causal = tri > 0.0
    q = q_ref[0].astype(F32) * (GLA_DK ** -0.5)
    k = k_ref[0].astype(F32)
    v = v_ref[0]
    zz = jnp.dot(g_ref[0].astype(BF16), wa_ref[...].astype(BF16), preferred_element_type=F32) + ba_ref[...]
    loga = _log_sigmoid(zz) * (1.0 / GLA_TAU)
    bcum = _split_dot(tri_b, loga, True)
    btot = _split_dot(ones_b, loga, True)
    qd = (q * jnp.exp(bcum)).astype(BF16)
    kd = (k * jnp.exp(-bcum)).astype(BF16)
    kdec = (k * jnp.exp(btot - bcum)).astype(BF16)
    dec = jnp.exp(btot)
    for hh in range(GLA_H):
        sl = slice(hh * LANE, (hh + 1) * LANE)
        qh, vh, kdh = qd[:, sl], v[:, sl], kdec[:, sl]
        att = lax.dot_general(qh, kd[:, sl], _NT, preferred_element_type=F32)
        att = jnp.where(causal, att, 0.0).astype(BF16)
        o_intra = jnp.dot(att, vh, preferred_element_type=F32)
        st = st_ref[hh]
        for c in _chunk_order(rev):
            rows = slice(c * CHUNK, (c + 1) * CHUNK)
            o_ref[0, rows, sl] = o_intra[rows] + lax.dot_general(qh[rows], st.astype(BF16), _NT,
                                                                 preferred_element_type=F32)
            st = st * dec[c * CHUNK:c * CHUNK + 1, sl] + lax.dot_general(vh[rows], kdh[rows], _TN,
                                                                         preferred_element_type=F32)
        st_ref[hh] = st


def _gla(rev, z, gates, wa, ba, tri, ones):
    b, lt, _ = z.shape
    nb = lt // TB
    d = int(rev)
    blk = lambda i: _scan_block(d, i, nb)
    zspec = lambda col: pl.BlockSpec((1, TB, HPAD), lambda bi, i: (bi, blk(i), col))
    const = lambda shape: pl.BlockSpec(shape, lambda bi, i: (0,) * len(shape))
    return pl.pallas_call(
        functools.partial(_gla_kernel, rev),
        grid=(b, nb),
        in_specs=[zspec(Z_GQ // HPAD), zspec(Z_GK // HPAD), zspec(Z_GV // HPAD),
                  pl.BlockSpec((1, TB, LANE), lambda bi, i: (bi, blk(i), 2 * d)),
                  const((LANE, HPAD)), const((1, HPAD)), const((TB, TB)), const((TB, TB))],
        out_specs=pl.BlockSpec((1, TB, HPAD), lambda bi, i: (bi, blk(i), 0)),
        out_shape=jax.ShapeDtypeStruct((b, lt, HPAD), F32),
        scratch_shapes=[pltpu.VMEM((GLA_H, LANE, LANE), F32)],
        compiler_params=_cparams(("arbitrary", "arbitrary")),
    )(z, z, z, gates, wa[d], ba[d], tri, ones)


def _mlstm_kernel(rev, q_ref, k_ref, v_ref, ga_ref, gb_ref, gat_ref, gbt_ref, ibr_ref, fbr_ref,
                  ibc_ref, fbc_ref, tri_ref, trit_ref, ones_ref, o_ref, ct_ref, m_ref):
    @pl.when(pl.program_id(1) == 0)
    def _():
        ct_ref[...] = jnp.zeros_like(ct_ref)
        m_ref[...] = jnp.zeros_like(m_ref)

    tri = tri_ref[...]
    tri_b = tri.astype(BF16)
    trit_b = trit_ref[...].astype(BF16)
    ones_b = ones_ref[...].astype(BF16)
    causal = tri > 0.0
    ones_col = lax.broadcasted_iota(jnp.int32, (TB, LANE), 1) == ML_D
    q = q_ref[0]
    k = k_ref[0].astype(F32) * (ML_D ** -0.5)
    v = v_ref[0]
    igc = ga_ref[0] + ibr_ref[...]
    lfc = _log_sigmoid(gb_ref[0] + fbr_ref[...])
    igr = gat_ref[0] + ibc_ref[...]
    lfr = _log_sigmoid(gbt_ref[0] + fbc_ref[...])
    fc = _split_dot(tri_b, lfc, True)
    ftot = _split_dot(ones_b, lfc, True)
    fr = _split_dot(trit_b, lfr, False)
    gc = ftot - fc + igc
    crow = lambda x, c: x[c * CHUNK:c * CHUNK + 1]
    m_loc = [jnp.max(gc[c * CHUNK:(c + 1) * CHUNK], axis=0, keepdims=True) for c in range(NCB)]
    m = m_ref[...]
    m_enter, a_c, b_c = [None] * NCB, [None] * NCB, [None] * NCB
    for c in _chunk_order(rev):
        m_enter[c] = m
        m_new = jnp.maximum(crow(ftot, c) + m, m_loc[c])
        a_c[c] = jnp.exp(crow(ftot, c) + m - m_new)
        b_c[c] = jnp.exp(m_loc[c] - m_new)
        m = m_new
    m_ref[...] = m
    spread = lambda rows_c: jnp.concatenate(
        [jnp.broadcast_to(r, (CHUNK, LANE)) for r in rows_c], axis=0)
    wc = jnp.exp(gc - spread(m_loc))
    inter_all = fc + spread(m_enter)
    for hh in range(ML_H):
        sl = slice(hh * LANE, (hh + 1) * LANE)
        gl = slice(GATE_OFF + hh, GATE_OFF + hh + 1)
        qh = q[:, sl]
        kh = k[:, sl]
        vaug = jnp.where(ones_col, 1.0, v[:, sl].astype(F32)).astype(BF16)
        qk = lax.dot_general(qh, kh.astype(BF16), _NT, preferred_element_type=F32)
        dlog = fc[:, gl] - fr[hh:hh + 1, :] + igr[hh:hh + 1, :]
        dlog = jnp.where(causal, dlog, -jnp.inf)
        inter = inter_all[:, gl]
        m_t = jnp.maximum(inter, jnp.max(dlog, axis=1, keepdims=True))
        w_inter = jnp.exp(inter - m_t)
        p = (jnp.exp(dlog - m_t) * qk).astype(BF16)
        nd_intra = jnp.dot(p, vaug, preferred_element_type=F32)
        floor = jnp.exp(-m_t)
        kw = (kh * wc[:, gl]).astype(BF16)
        ct = ct_ref[hh]
        for c in _chunk_order(rev):
            rows = slice(c * CHUNK, (c + 1) * CHUNK)
            nd = nd_intra[rows] + w_inter[rows] * lax.dot_general(qh[rows], ct.astype(BF16), _NT,
                                                                  preferred_element_type=F32)
            den = nd[:, ML_D:ML_D + 1]
            o_ref[0, rows, sl] = nd / jnp.maximum(jnp.abs(den), floor[rows])
            ct = a_c[c][:, gl] * ct + b_c[c][:, gl] * lax.dot_general(vaug[rows], kw[rows], _TN,
                                                                      preferred_element_type=F32)
        ct_ref[hh] = ct


def _mlstm(rev, z, gates, gates_t, ibr, fbr, ibc, fbc, tri, trit, ones):
    b, lt, _ = z.shape
    nb = lt // TB
    d = int(rev)
    blk = lambda i: _scan_block(d, i, nb)
    zspec = lambda col: pl.BlockSpec((1, TB, HPAD), lambda bi, i: (bi, blk(i), col))
    gspec = lambda t: pl.BlockSpec((1, TB, LANE), lambda bi, i: (bi, blk(i), 2 * d + t))
    gtspec = lambda t: pl.BlockSpec((1, 8, TB), lambda bi, i: (bi, 2 * d + t, blk(i)))
    const = lambda shape: pl.BlockSpec(shape, lambda bi, i: (0,) * len(shape))
    return pl.pallas_call(
        functools.partial(_mlstm_kernel, rev),
        grid=(b, nb),
        in_specs=[zspec(Z_MQ // HPAD), zspec(Z_MK // HPAD), zspec(Z_MV // HPAD),
                  gspec(0), gspec(1), gtspec(0), gtspec(1),
                  const((1, LANE)), const((1, LANE)), const((8, 1)), const((8, 1)),
                  const((TB, TB)), const((TB, TB)), const((TB, TB))],
        out_specs=pl.BlockSpec((1, TB, HPAD), lambda bi, i: (bi, blk(i), 0)),
        out_shape=jax.ShapeDtypeStruct((b, lt, HPAD), F32),
        scratch_shapes=[pltpu.VMEM((ML_H, LANE, LANE), F32), pltpu.VMEM((1, LANE), F32)],
        compiler_params=_cparams(("arbitrary", "arbitrary")),
    )(z, z, z, gates, gates, gates_t, gates_t, ibr[d], fbr[d], ibc[d], fbc[d], tri, trit, ones)


def _s5_prep_kernel(arc_ref, aic_ref, arr_ref, air_ref, ldt_ref, br_ref, bi_ref, cr_ref, ci_ref,
                    crt_ref, cit_ref, ar_out, ai_out, k_out, k0_out, vr_out, vi_out, l_out):
    nj = S5_T * S5_HC
    jmap = (lax.broadcasted_iota(jnp.int32, (1, nj), 1) // S5_HC).astype(F32)
    fmap = (lax.broadcasted_iota(jnp.int32, (nj, 1), 0) // S5_HC + 1).astype(F32)
    kmats = []
    for d in range(2):
        dt = jnp.exp(ldt_ref[0, d])
        a_re, a_im = arc_ref[0, d], aic_ref[0, d]
        arc, aic = a_re * dt, a_im * dt
        mag = jnp.exp(arc)
        lre, lim = mag * jnp.cos(aic), mag * jnp.sin(aic)
        den = a_re * a_re + a_im * a_im
        z_re = ((lre - 1.0) * a_re + lim * a_im) / den
        z_im = (lim * a_re - (lre - 1.0) * a_im) / den
        b_re, b_im = br_ref[0, d], bi_ref[0, d]
        bb_re = z_re * b_re - z_im * b_im
        bb_im = z_re * b_im + z_im * b_re
        pmag = jnp.exp(arc * jmap)
        p_re, p_im = pmag * jnp.cos(aic * jmap), pmag * jnp.sin(aic * jmap)
        a_st_re = p_re * bb_re - p_im * bb_im
        a_st_im = p_re * bb_im + p_im * bb_re
        ar_out[0, d] = a_st_re
        ai_out[0, d] = a_st_im
        kmat = (jnp.dot(cr_ref[0, d], a_st_re, preferred_element_type=F32, precision=HIGHEST)
                - jnp.dot(ci_ref[0, d], a_st_im, preferred_element_type=F32, precision=HIGHEST))
        k_out[0, d] = kmat
        kmats.append(kmat)
        arr, air = arr_ref[0, d] * dt, air_ref[0, d] * dt
        fmag = jnp.exp(arr * fmap)
        f_re, f_im = fmag * jnp.cos(air * fmap), fmag * jnp.sin(air * fmap)
        c_re, c_im = crt_ref[0, d], cit_ref[0, d]
        vr_out[0, d] = c_re * f_re - c_im * f_im
        vi_out[0, d] = -(c_re * f_im + c_im * f_re)
        tmag = jnp.exp(arr * float(S5_T))
        t_re, t_im = tmag * jnp.cos(air * float(S5_T)), tmag * jnp.sin(air * float(S5_T))
        l_out[0, d, 0:1, 0:S5_P] = t_re
        l_out[0, d, 0:1, S5_P:2 * S5_P] = t_re
        l_out[0, d, 1:2, 0:S5_P] = -t_im
        l_out[0, d, 1:2, S5_P:2 * S5_P] = t_im
    k0_out[0] = kmats[0][:, 0:S5_HC] + kmats[1][:, 0:S5_HC]


def _s5_operators(a_re, a_im, log_dt, b_re, b_im, c_re, c_im):
    g, p, hc, t = S5_G, S5_P, S5_HC, S5_T
    nj = t * hc
    gm = lambda x: jnp.moveaxis(x, 0, 1)
    spec = lambda shape: pl.BlockSpec((1,) + shape, lambda i: (i,) + (0,) * len(shape))
    outs = pl.pallas_call(
        _s5_prep_kernel,
        grid=(g,),
        in_specs=[spec((2, p, 1)), spec((2, p, 1)), spec((2, 1, p)), spec((2, 1, p)), spec((2, 1, 1)),
                  spec((2, p, nj)), spec((2, p, nj)), spec((2, hc, p)), spec((2, hc, p)),
                  spec((2, nj, p)), spec((2, nj, p))],
        out_specs=[spec((2, p, nj)), spec((2, p, nj)), spec((2, hc, nj)), spec((hc, hc)),
                   spec((2, nj, p)), spec((2, nj, p)), spec((2, 2, 2 * p))],
        out_shape=[jax.ShapeDtypeStruct((g, 2, p, nj), F32), jax.ShapeDtypeStruct((g, 2, p, nj), F32),
                   jax.ShapeDtypeStruct((g, 2, hc, nj), F32), jax.ShapeDtypeStruct((g, hc, hc), F32),
                   jax.ShapeDtypeStruct((g, 2, nj, p), F32), jax.ShapeDtypeStruct((g, 2, nj, p), F32),
                   jax.ShapeDtypeStruct((g, 2, 2, 2 * p), F32)],
        compiler_params=_cparams(("arbitrary",)),
    )(gm(a_re)[..., None], gm(a_im)[..., None], gm(a_re)[:, :, None, :], gm(a_im)[:, :, None, :],
      gm(log_dt)[..., None, None],
      jnp.tile(gm(b_re), (1, 1, 1, t)), jnp.tile(gm(b_im), (1, 1, 1, t)), gm(c_re), gm(c_im),
      jnp.tile(gm(c_re), (1, 1, t, 1)), jnp.tile(gm(c_im), (1, 1, t, 1)))
    a_st_re, a_st_im, kmat, k0, v_re, v_im, lam_t = outs

    a5 = jnp.stack([a_st_re, a_st_im], axis=2).reshape(g, 2, 2, p, t, hc)
    a5 = jnp.stack([jnp.flip(a5[:, 0], axis=3), a5[:, 1]], axis=1)
    wcat = jnp.transpose(a5, (0, 4, 5, 1, 2, 3)).reshape(g, nj, 4 * p)
    v5 = jnp.stack([v_re, v_im], axis=2).reshape(g, 2, 2, t, hc, p)
    v5 = jnp.stack([v5[:, 0], jnp.flip(v5[:, 1], axis=2)], axis=1)
    vcat = jnp.transpose(v5, (0, 1, 2, 5, 3, 4)).reshape(g, 4 * p, nj)
    k5 = kmat.reshape(g, 2, hc, t, hc)
    ss = np.arange(t)[:, None]
    tt = np.arange(t)[None, :]
    lag = np.abs(tt - ss)
    kf = k5[:, 0][:, :, lag, :]
    kb = k5[:, 1][:, :, lag, :]
    sel_f = jnp.asarray(tt > ss)[None, None, :, :, None]
    sel_b = jnp.asarray(ss > tt)[None, None, :, :, None]
    m5 = jnp.where(sel_f, kf, jnp.where(sel_b, kb, k0[:, :, None, None, :]))
    mfb = jnp.transpose(m5, (0, 2, 4, 3, 1)).reshape(g, nj, nj)
    return wcat.astype(BF16), vcat.astype(BF16), mfb.astype(BF16), lam_t


S5_CB = TB // S5_T
S5_GH = LANE // S5_HC


def _s5_group_rows(ua_ref, ub_ref, ug_ref):
    nbat = ua_ref.shape[0]
    for t in range(S5_T):
        for half, ref in enumerate((ua_ref, ub_ref)):
            rows = jnp.concatenate([ref[bi, pl.ds(t, S5_CB, stride=S5_T), :] for bi in range(nbat)], axis=0)
            for gg in range(S5_GH):
                ug_ref[half * S5_GH + gg, :, t * S5_HC:(t + 1) * S5_HC] = rows[:, gg * S5_HC:(gg + 1) * S5_HC]


def _s5_dx_kernel(ua_ref, ub_ref, w_ref, dxf_ref, dxb_ref, ug_ref):
    nbat = ua_ref.shape[0]
    _s5_group_rows(ua_ref, ub_ref, ug_ref)
    for g in range(S5_G):
        res = jnp.dot(ug_ref[g].astype(BF16), w_ref[g], preferred_element_type=F32)
        for bi in range(nbat):
            rows = slice(bi * S5_CB, (bi + 1) * S5_CB)
            dxf_ref[bi, pl.ds(g, S5_CB, stride=S5_G), :] = res[rows, 0:LANE]
            dxb_ref[bi, pl.ds(g, S5_CB, stride=S5_G), :] = res[rows, LANE:2 * LANE]


def _s5_scan_kernel(dxf_ref, dxb_ref, l_ref, xef_ref, xeb_ref, sf_ref, sb_ref):
    nbat = dxf_ref.shape[0]

    @pl.when(pl.program_id(0) == 0)
    def _():
        sf_ref[...] = jnp.zeros_like(sf_ref)
        sb_ref[...] = jnp.zeros_like(sb_ref)

    tile = lambda r: jnp.concatenate([l_ref[r]] * nbat, axis=0)
    la_f, lb_f, la_b, lb_b = tile(0), tile(1), tile(2), tile(3)

    def step(j, carry):
        xf, xb = carry
        rf = pl.ds(pl.multiple_of(j * S5_G, S5_G), S5_G)
        rb = pl.ds(pl.multiple_of((S5_CB - 1 - j) * S5_G, S5_G), S5_G)
        xef_ref[:, rf, :] = xf.reshape(nbat, S5_G, LANE)
        xeb_ref[:, rb, :] = xb.reshape(nbat, S5_G, LANE)
        dxf = dxf_ref[:, rf, :].reshape(nbat * S5_G, LANE)
        dxb = dxb_ref[:, rb, :].reshape(nbat * S5_G, LANE)
        xf = la_f * xf + lb_f * pltpu.roll(xf, S5_P, 1) + dxf
        xb = la_b * xb + lb_b * pltpu.roll(xb, S5_P, 1) + dxb
        return xf, xb

    xf, xb = lax.fori_loop(0, S5_CB, step, (sf_ref[...], sb_ref[...]))
    sf_ref[...] = xf
    sb_ref[...] = xb


def _s5_y_kernel(ua_ref, ub_ref, xef_ref, xeb_ref, m_ref, v_ref, ya_ref, yb_ref, ug_ref, yg_ref):
    nbat = ua_ref.shape[0]
    _s5_group_rows(ua_ref, ub_ref, ug_ref)
    for g in range(S5_G):
        grows = lambda ref: jnp.concatenate(
            [ref[bi, pl.ds(g, S5_CB, stride=S5_G), :] for bi in range(nbat)], axis=0)
        xe = jnp.concatenate([grows(xef_ref), grows(xeb_ref)], axis=1).astype(BF16)
        yg_ref[g] = (jnp.dot(ug_ref[g].astype(BF16), m_ref[g], preferred_element_type=F32)
                     + jnp.dot(xe, v_ref[g], preferred_element_type=F32))
    for t in range(S5_T):
        for half, ref in enumerate((ya_ref, yb_ref)):
            rows = jnp.concatenate([yg_ref[half * S5_GH + gg, :, t * S5_HC:(t + 1) * S5_HC]
                                    for gg in range(S5_GH)], axis=1)
            for bi in range(nbat):
                ref[bi, pl.ds(t, S5_CB, stride=S5_T), :] = rows[bi * S5_CB:(bi + 1) * S5_CB]


def _s5(ua, ub, wcat, vcat, mfb, lam_t):
    b, lt, _ = ua.shape
    g, nj = S5_G, S5_T * S5_HC
    nb = lt // TB
    srows = S5_CB * g
    tok = pl.BlockSpec((b, TB, LANE), lambda i: (0, i, 0))
    full = lambda a: pl.BlockSpec(a.shape, lambda i: (0,) * a.ndim)
    st_f = pl.BlockSpec((b, srows, LANE), lambda i: (0, i, 0))
    st_b = pl.BlockSpec((b, srows, LANE), lambda i: (0, _scan_block(1, i, nb), 0))
    st_shape = jax.ShapeDtypeStruct((b, (lt // S5_T) * g, LANE), F32)
    dxf, dxb = pl.pallas_call(
        _s5_dx_kernel,
        grid=(nb,),
        in_specs=[tok, tok, full(wcat)],
        out_specs=[st_f, st_f],
        out_shape=[st_shape, st_shape],
        scratch_shapes=[pltpu.VMEM((g, b * S5_CB, nj), F32)],
        compiler_params=_cparams(("arbitrary",)),
    )(ua, ub, wcat)
    lam4 = jnp.moveaxis(lam_t.reshape(g, 4, LANE), 1, 0)
    xef, xeb = pl.pallas_call(
        _s5_scan_kernel,
        grid=(nb,),
        in_specs=[st_f, st_b, full(lam4)],
        out_specs=[st_f, st_b],
        out_shape=[st_shape, st_shape],
        scratch_shapes=[pltpu.VMEM((b * g, LANE), F32)] * 2,
        compiler_params=_cparams(("arbitrary",)),
    )(dxf, dxb, lam4)
    return pl.pallas_call(
        _s5_y_kernel,
        grid=(nb,),
        in_specs=[tok, tok, st_f, st_f, full(mfb), full(vcat)],
        out_specs=[tok, tok],
        out_shape=[jax.ShapeDtypeStruct((b, lt, LANE), F32)] * 2,
        scratch_shapes=[pltpu.VMEM((g, b * S5_CB, nj), F32)] * 2,
        compiler_params=_cparams(("arbitrary",)),
    )(ua, ub, xef, xeb, mfb, vcat)


def _head_ln(o, gain):
    valid = lax.broadcasted_iota(jnp.int32, (1, LANE), 1) < GLA_DV
    outs = []
    for hh in range(GLA_H):
        oh = o[:, hh * LANE:(hh + 1) * LANE]
        mu = jnp.sum(jnp.where(valid, oh, 0.0), axis=-1, keepdims=True) * (1.0 / GLA_DV)
        oc = jnp.where(valid, oh - mu, 0.0)
        var = jnp.sum(oc * oc, axis=-1, keepdims=True) * (1.0 / GLA_DV)
        outs.append(oc * lax.rsqrt(var + LN_EPS))
    return jnp.concatenate(outs, axis=-1) * gain


def _out_kernel(alpha, h_ref, mod_ref, ya_ref, yb_ref, ua_ref, ub_ref, gr_ref, mo_ref, gf_ref, gb_ref,
                mf_ref, mb_ref, dsk_ref, wglu_ref, bglu_ref, gg_ref, mg_ref, wout_ref, l1g_ref, l1b_ref,
                wup_ref, h1_ref, a_ref, v_ref, mix_ref):
    mod = mod_ref[0, 0]
    ys = jnp.concatenate([ya_ref[0], yb_ref[0]], axis=1)
    u = jnp.concatenate([ua_ref[0], ub_ref[0]], axis=1)
    y1 = jax.nn.gelu(ys + dsk_ref[...] * u)
    glu = jnp.dot(y1.astype(BF16), wglu_ref[...], preferred_element_type=F32) + bglu_ref[...]
    mix_ref[:, 0:S5_W] = (y1 * jax.nn.sigmoid(glu)).astype(BF16)
    r = gr_ref[0].astype(F32)
    glo = _head_ln(gf_ref[0] + gb_ref[0], gg_ref[...])
    mix_ref[:, S5_W:S5_W + HPAD] = (r * jax.nn.sigmoid(r) * glo).astype(BF16)
    mlo = _head_ln(mf_ref[0] + mb_ref[0], mg_ref[...])
    mix_ref[:, S5_W + HPAD:MIX_W] = (jax.nn.sigmoid(mo_ref[0].astype(F32)) * mlo).astype(BF16)
    mixed = jnp.dot(mix_ref[...], wout_ref[...], preferred_element_type=F32)
    y = alpha * h_ref[0] + mod[2:3] * mixed
    h1 = _ln_rows(y) * l1g_ref[...] + l1b_ref[...]
    h1_ref[0] = h1
    hm = (_ln_rows(h1) * (1.0 + mod[4:5]) + mod[3:4]).astype(BF16)
    a_ref[0] = jnp.dot(hm, wup_ref[:, 0:D_FF], preferred_element_type=F32).astype(BF16)
    v_ref[0] = jnp.dot(hm, wup_ref[:, D_FF:2 * D_FF], preferred_element_type=F32).astype(BF16)


def _out_proj(alpha, h, modtab, ys5, us5, z, ogla, oml, dsk, wglu, bglu, gg, mg, wout, l1g, l1b, wup):
    b, lt, d = h.shape
    nb = lt // TB
    const = lambda shape: pl.BlockSpec(shape, lambda bi, j: (0,) * len(shape))
    tok = lambda w, col=0: pl.BlockSpec((1, TB, w), lambda bi, j: (bi, j, col))
    return pl.pallas_call(
        functools.partial(_out_kernel, alpha),
        grid=(b, nb),
        in_specs=[tok(d), pl.BlockSpec((1, 1, 6, d), lambda bi, j: (bi, jnp.minimum(j, 1), 0, 0)),
                  tok(LANE), tok(LANE), tok(LANE), tok(LANE), tok(HPAD, Z_GR // HPAD), tok(HPAD, Z_MO // HPAD),
                  tok(HPAD), tok(HPAD), tok(HPAD), tok(HPAD),
                  const((1, S5_W)), const((S5_W, S5_W)), const((1, S5_W)), const((1, HPAD)), const((1, HPAD)),
                  const((MIX_W, d)), const((1, d)), const((1, d)), const((d, 2 * D_FF))],
        out_specs=[tok(d), tok(D_FF), tok(D_FF)],
        out_shape=[jax.ShapeDtypeStruct((b, lt, d), F32), jax.ShapeDtypeStruct((b, lt, D_FF), BF16),
                   jax.ShapeDtypeStruct((b, lt, D_FF), BF16)],
        scratch_shapes=[pltpu.VMEM((TB, MIX_W), BF16)],
        compiler_params=_cparams(("arbitrary", "arbitrary")),
    )(h, modtab, ys5[0], ys5[1], us5[0], us5[1], z, z, ogla[0], ogla[1], oml[0], oml[1], dsk, wglu, bglu,
      gg, mg, wout, l1g, l1b, wup)


FF_CW = 256


def _ffn_kernel(alpha, nb, h_ref, mod_ref, a_ref, ap_ref, an_ref, v_ref, wc_ref, bc_ref, wd_ref,
                l2g_ref, l2b_ref, o_ref, g_ref):
    j = pl.program_id(1)
    mod = mod_ref[0, 0]
    is_lat = j > 0
    t_idx = lax.broadcasted_iota(jnp.int32, (TB, 1), 0)
    col = jnp.where(is_lat, t_idx % GRID_W, t_idx)
    last = jnp.where(is_lat, GRID_W - 1, TB - 1)
    has_l = (col > 0).astype(F32)
    has_r = (col < last).astype(F32)
    vert = is_lat.astype(F32)
    up_ok = (j > 1).astype(F32) * vert
    dn_ok = (j < nb - 1).astype(F32) * vert
    for c0 in range(0, D_FF, FF_CW):
        cs = slice(c0, c0 + FF_CW)
        a = a_ref[0, :, cs].astype(F32)
        prev = ap_ref[0, :, cs].astype(F32) * up_ok
        nxt = an_ref[0, :, cs].astype(F32) * dn_ok
        up = jnp.concatenate([prev, a[:TB - GRID_W]], axis=0) * vert
        dn = jnp.concatenate([a[GRID_W:], nxt], axis=0) * vert
        w = wc_ref[:, cs]
        acc = bc_ref[:, cs] + a * w[4:5]
        acc = acc + up * w[1:2] + dn * w[7:8]
        left = (pltpu.roll(up, 1, 0) * w[0:1] + pltpu.roll(a, 1, 0) * w[3:4]
                + pltpu.roll(dn, 1, 0) * w[6:7])
        right = (pltpu.roll(up, TB - 1, 0) * w[2:3] + pltpu.roll(a, TB - 1, 0) * w[5:6]
                 + pltpu.roll(dn, TB - 1, 0) * w[8:9])
        acc = acc + has_l * left + has_r * right
        g_ref[:, cs] = (jax.nn.gelu(acc) * v_ref[0, :, cs].astype(F32)).astype(BF16)
    f = jnp.dot(g_ref[...], wd_ref[...], preferred_element_type=F32)
    y = alpha * h_ref[0] + mod[5:6] * f
    o_ref[0] = _ln_rows(y) * l2g_ref[...] + l2b_ref[...]


def _ffn(alpha, h1, modtab, a, v, wc, bc, wd, l2g, l2b):
    b, lt, d = h1.shape
    nb = lt // TB
    rpb = TB // GRID_W
    nrow = lt // GRID_W
    const = lambda shape: pl.BlockSpec(shape, lambda bi, j: (0,) * len(shape))
    tok = lambda w: pl.BlockSpec((1, TB, w), lambda bi, j: (bi, j, 0))
    return pl.pallas_call(
        functools.partial(_ffn_kernel, alpha, nb),
        grid=(b, nb),
        in_specs=[tok(d), pl.BlockSpec((1, 1, 6, d), lambda bi, j: (bi, jnp.minimum(j, 1), 0, 0)),
                  tok(D_FF),
                  pl.BlockSpec((1, GRID_W, D_FF), lambda bi, j: (bi, jnp.maximum(j * rpb - 1, 0), 0)),
                  pl.BlockSpec((1, GRID_W, D_FF), lambda bi, j: (bi, jnp.minimum((j + 1) * rpb, nrow - 1), 0)),
                  tok(D_FF), const((9, D_FF)), const((1, D_FF)), const((D_FF, d)),
                  const((1, d)), const((1, d))],
        out_specs=tok(d),
        out_shape=jax.ShapeDtypeStruct((b, lt, d), F32),
        scratch_shapes=[pltpu.VMEM((TB, D_FF), BF16)],
        compiler_params=_cparams(("arbitrary", "arbitrary")),
    )(h1, modtab, a, a, a, v, wc, bc, wd, l2g, l2b)


def _pad_heads(w, nh, dh):
    lead = w.shape[:-1]
    w = w.reshape(lead + (nh, dh))
    w = jnp.pad(w, [(0, 0)] * len(lead) + [(0, 0), (0, LANE - dh)])
    return w.reshape(lead + (nh * LANE,))


def _pad_head_rows(w, nh, dh):
    return jnp.moveaxis(_pad_heads(jnp.moveaxis(w, 0, -1), nh, dh), -1, 0)


def _layer_weights(w_in, gla_w_a2, gla_b_a, ml_i_bias, ml_f_bias, w_out):
    d = w_in.shape[0]
    offs = np.cumsum((0, S5_W, GLA_H * GLA_DK, GLA_H * GLA_DK, GLA_W, GLA_W, 2 * GLA_RANK,
                      ML_W, ML_W, ML_W, ML_W, 2 * ML_H, 2 * ML_H))
    part = lambda i: w_in[:, offs[i]:offs[i + 1]]
    w_p = jnp.concatenate([
        _pad_heads(part(1), GLA_H, GLA_DK), _pad_heads(part(2), GLA_H, GLA_DK),
        _pad_heads(part(3), GLA_H, GLA_DV), _pad_heads(part(4), GLA_H, GLA_DV),
        _pad_heads(part(6), ML_H, ML_D), _pad_heads(part(7), ML_H, ML_D),
        _pad_heads(part(8), ML_H, ML_D), _pad_heads(part(9), ML_H, ML_D)], axis=1).astype(BF16)
    w_u = part(0).astype(BF16)
    lr, ig, fg = part(5), part(10), part(11)
    zeros = lambda n: jnp.zeros((d, n), w_in.dtype)
    tiles = []
    for dd in range(2):
        tiles += [lr[:, dd * GLA_RANK:(dd + 1) * GLA_RANK], ig[:, dd * ML_H:(dd + 1) * ML_H],
                  zeros(LANE - GLA_RANK - ML_H),
                  zeros(GATE_OFF), fg[:, dd * ML_H:(dd + 1) * ML_H], zeros(LANE - GATE_OFF - ML_H)]
    w_g = jnp.concatenate(tiles, axis=1).astype(BF16)
    rows_t = []
    for dd in range(2):
        for gate in (ig, fg):
            rows_t += [gate[:, dd * ML_H:(dd + 1) * ML_H], zeros(8 - ML_H)]
    w_gt = jnp.concatenate(rows_t, axis=1).T.astype(BF16)
    wa = jnp.pad(_pad_heads(gla_w_a2, GLA_H, GLA_DK), ((0, 0), (0, LANE - GLA_RANK), (0, 0)))
    ba = _pad_heads(gla_b_a, GLA_H, GLA_DK)[:, None, :]
    gate_row = lambda bias: jnp.pad(bias, ((0, 0), (GATE_OFF, LANE - GATE_OFF - ML_H)))[:, None, :]
    gate_col = lambda bias: jnp.pad(bias, ((0, 0), (0, 8 - ML_H)))[:, :, None]
    wout_p = jnp.concatenate([
        w_out[:S5_W], _pad_head_rows(w_out[S5_W:S5_W + GLA_W], GLA_H, GLA_DV),
        _pad_head_rows(w_out[S5_W + GLA_W:], ML_H, ML_D)], axis=0).astype(BF16)
    return dict(w_p=w_p, w_u=w_u, w_g=w_g, w_gt=w_gt, wa=wa, ba=ba,
                ibr=gate_row(ml_i_bias), fbr=gate_row(ml_f_bias),
                ibc=gate_col(ml_i_bias), fbc=gate_col(ml_f_bias), wout=wout_p)


def kernel(x, c, ctx, c_ctx, w_ada, b_ada, w_in, s5_a_re, s5_a_im, s5_log_dt, s5_b_re, s5_b_im, s5_c_re, s5_c_im, s5_d, s5_w_glu, s5_b_glu, gla_w_a2, gla_b_a, gla_g, ml_i_bias, ml_f_bias, ml_g, w_out, ln1_g, ln1_b, w_up, w_dconv, b_dconv, w_down, ln2_g, ln2_b):
    b, seq, d = x.shape
    ctx_len = ctx.shape[1]
    depth = w_in.shape[0]
    assert d == D_MODEL and ctx_len == TB and seq % TB == 0 and b <= 7
    alpha = (2.0 * depth) ** 0.25

    cc = jnp.zeros((8, d), F32).at[:b].set(c).at[b].set(c_ctx)
    mod = _modulation(cc, w_ada, b_ada).reshape(depth, 8, 6, d)
    modtab = jnp.stack([jnp.broadcast_to(mod[:, b:b + 1], (depth, b, 6, d)), mod[:, :b]], axis=2)

    eye = np.eye(NCB, dtype=np.float32)
    low = np.kron(eye, np.tril(np.ones((CHUNK, CHUNK), np.float32)))
    tri = (jnp.asarray(low), jnp.asarray(low.T))
    ones = jnp.asarray(np.kron(eye, np.ones((CHUNK, CHUNK), np.float32)))

    h = jnp.concatenate([ctx, x], axis=1)
    for l in range(depth):
        lw = _layer_weights(w_in[l], gla_w_a2[l], gla_b_a[l], ml_i_bias[l], ml_f_bias[l], w_out[l])
        z, ua, ub, gates, gates_t = _in_proj(h, modtab[l], lw["w_p"], lw["w_u"], lw["w_g"], lw["w_gt"])
        ogla = [_gla(rev, z, gates, lw["wa"], lw["ba"], tri[rev], ones) for rev in (0, 1)]
        oml = [_mlstm(rev, z, gates, gates_t, lw["ibr"], lw["fbr"], lw["ibc"], lw["fbc"],
                      tri[rev], tri[1 - rev], ones) for rev in (0, 1)]
        wcat, vcat, mfb, lam_t = _s5_operators(s5_a_re[l], s5_a_im[l], s5_log_dt[l], s5_b_re[l],
                                               s5_b_im[l], s5_c_re[l], s5_c_im[l])
        ys5 = _s5(ua, ub, wcat, vcat, mfb, lam_t)
        h1, a, v = _out_proj(alpha, h, modtab[l], ys5, (ua, ub), z, ogla, oml, s5_d[l][None],
                             s5_w_glu[l].astype(BF16),
                             s5_b_glu[l][None], _pad_heads(gla_g[l], GLA_H, GLA_DV)[None],
                             _pad_heads(ml_g[l], ML_H, ML_D)[None], lw["wout"], ln1_g[l][None],
                             ln1_b[l][None], w_up[l].astype(BF16))
        h = _ffn(alpha, h1, modtab[l], a, v, w_dconv[l].reshape(9, D_FF), b_dconv[l][None],
                 w_down[l].astype(BF16), ln2_g[l][None], ln2_b[l][None])
    return h[:, ctx_len:]
```

```python
import functools
import math

import numpy as np
import jax
import jax.numpy as jnp
from jax import lax
from jax.experimental import pallas as pl
from jax.experimental.pallas import tpu as pltpu

F32 = jnp.float32
BF16 = jnp.bfloat16
HIGHEST = lax.Precision.HIGHEST

D_MODEL = 1024
GRID_W = 64
CHUNK = 64
S5_W = D_MODEL // 4
S5_HC = 16
S5_G = S5_W // S5_HC
S5_P = 64
S5_T = 16
GLA_W = 3 * D_MODEL // 8
GLA_H = 4
GLA_DV = GLA_W // GLA_H
GLA_DK = GLA_DV // 2
GLA_RANK = 16
GLA_TAU = 16.0
ML_W = 3 * D_MODEL // 8
ML_H = 4
ML_D = ML_W // ML_H
D_FF = ((8 * D_MODEL // 3 + 127) // 128) * 128
LN_EPS = 1e-5

LANE = 128
HPAD = GLA_H * LANE
TB = 256
NCB = TB // CHUNK
Z_GQ, Z_GK, Z_GV, Z_GR = 0, HPAD, 2 * HPAD, 3 * HPAD
Z_MQ, Z_MK, Z_MV, Z_MO = 4 * HPAD, 5 * HPAD, 6 * HPAD, 7 * HPAD
NZM = 8 * HPAD
NGATE = 4 * LANE
NGATE_T = 32
GATE_OFF = 16
MIX_W = S5_W + 2 * HPAD

VMEM_LIMIT = 56 * 1024 * 1024


def _cparams(sem):
    return pltpu.CompilerParams(dimension_semantics=sem, vmem_limit_bytes=VMEM_LIMIT)


def _ln_rows(x):
    mu = jnp.mean(x, axis=-1, keepdims=True)
    xc = x - mu
    var = jnp.mean(xc * xc, axis=-1, keepdims=True)
    return xc * lax.rsqrt(var + LN_EPS)


def _log_sigmoid(x):
    return jnp.minimum(x, 0.0) - jnp.log(1.0 + jnp.exp(-jnp.abs(x)))


def _split_dot(tri_b, x, left):
    hi = x.astype(BF16)
    lo = (x - hi.astype(F32)).astype(BF16)
    if left:
        return (jnp.dot(tri_b, hi, preferred_element_type=F32)
                + jnp.dot(tri_b, lo, preferred_element_type=F32))
    return (jnp.dot(hi, tri_b, preferred_element_type=F32)
            + jnp.dot(lo, tri_b, preferred_element_type=F32))


_NT = (((1,), (1,)), ((), ()))
_TN = (((0,), (0,)), ((), ()))


def _mod_kernel(c_ref, w_ref, b_ref, o_ref):
    cc = c_ref[...]
    s = cc * jax.nn.sigmoid(cc)
    o_ref[0] = jnp.dot(s, w_ref[0], preferred_element_type=F32, precision=HIGHEST) + b_ref[0]


def _modulation(cc, w_ada, b_ada):
    depth, d, n = w_ada.shape
    nb = 4
    bn = n // nb
    return pl.pallas_call(
        _mod_kernel,
        grid=(depth, nb),
        in_specs=[pl.BlockSpec((8, d), lambda l, j: (0, 0)),
                  pl.BlockSpec((1, d, bn), lambda l, j: (l, 0, j)),
                  pl.BlockSpec((1, 1, bn), lambda l, j: (l, 0, j))],
        out_specs=pl.BlockSpec((1, 8, bn), lambda l, j: (l, 0, j)),
        out_shape=jax.ShapeDtypeStruct((depth, 8, n), F32),
        compiler_params=_cparams(("arbitrary", "arbitrary")),
    )(cc, w_ada, b_ada.reshape(depth, 1, n))


def _stream_operands(h):
    if isinstance(h, tuple):
        ctx, x = h
        total = ctx.shape[1] + x.shape[1]
        run = pl.BlockSpec((1, TB, x.shape[2]), lambda bi, j: (bi, jnp.maximum(j - 1, 0), 0))
    else:
        ctx = x = h
        total = x.shape[1]
        run = pl.BlockSpec((1, TB, x.shape[2]), lambda bi, j: (bi, j, 0))
    head = pl.BlockSpec((1, TB, x.shape[2]), lambda bi, j: (bi, 0, 0))
    return (ctx, x), [head, run], (x.shape[0], total)


def _stream_block(hc_ref, hx_ref):
    return jnp.where(pl.program_id(1) == 0, hc_ref[0], hx_ref[0])


def _in_kernel(hc_ref, hx_ref, mod_ref, w_ref, wu_ref, wg_ref, wgt_ref, z_ref, ua_ref, ub_ref, g_ref, gt_ref):
    x = _stream_block(hc_ref, hx_ref)
    mod = mod_ref[0, 0]
    hm = (_ln_rows(x) * (1.0 + mod[1:2]) + mod[0:1]).astype(BF16)
    for c0 in range(0, NZM, HPAD):
        z_ref[0, :, c0:c0 + HPAD] = jnp.dot(hm, w_ref[:, c0:c0 + HPAD],
                                            preferred_element_type=F32).astype(BF16)
    u = jnp.dot(hm, wu_ref[...], preferred_element_type=F32)
    ua_ref[0] = u[:, :LANE]
    ub_ref[0] = u[:, LANE:]
    g_ref[0] = jnp.dot(hm, wg_ref[...], preferred_element_type=F32)
    gt_ref[0] = lax.dot_general(wgt_ref[...], hm, _NT, preferred_element_type=F32)


def _in_proj(h, modtab, w_p, w_u, w_g, w_gt):
    h_ops, h_specs, (b, lt) = _stream_operands(h)
    d = D_MODEL
    nb = lt // TB
    const = lambda shape: pl.BlockSpec(shape, lambda bi, j: (0,) * len(shape))
    tok = lambda w: pl.BlockSpec((1, TB, w), lambda bi, j: (bi, j, 0))
    return pl.pallas_call(
        _in_kernel,
        grid=(b, nb),
        in_specs=h_specs + [
                  pl.BlockSpec((1, 1, 6, d), lambda bi, j: (bi, jnp.minimum(j, 1), 0, 0)),
                  const((d, NZM)), const((d, S5_W)), const((d, NGATE)), const((NGATE_T, d))],
        out_specs=[tok(NZM), tok(LANE), tok(LANE), tok(NGATE),
                   pl.BlockSpec((1, NGATE_T, TB), lambda bi, j: (bi, 0, j))],
        out_shape=[jax.ShapeDtypeStruct((b, lt, NZM), BF16),
                   jax.ShapeDtypeStruct((b, lt, LANE), F32),
                   jax.ShapeDtypeStruct((b, lt, LANE), F32),
                   jax.ShapeDtypeStruct((b, lt, NGATE), F32),
                   jax.ShapeDtypeStruct((b, NGATE_T, lt), F32)],
        compiler_params=_cparams(("arbitrary", "arbitrary")),
    )(*h_ops, modtab, w_p, w_u, w_g, w_gt)


def _scan_block(d, i, nb):
    return jnp.where(d == 0, i, jnp.where(i == 0, 0, nb - i))


MIXER_STREAMS = 4


def _mixer_streams(b):
    return MIXER_STREAMS if b % MIXER_STREAMS == 0 else 1


def _chunk_order(rev):
    return range(NCB - 1, -1, -1) if rev else range(NCB)


def _chunk_totals(rev, cum):
    last = 0 if rev else CHUNK - 1
    return jnp.concatenate(
        [jnp.broadcast_to(cum[c * CHUNK + last:c * CHUNK + last + 1], (CHUNK, cum.shape[1]))
         for c in range(NCB)], axis=0)


def _gla_kernel(rev, q_ref, k_ref, v_ref, g_ref, wa_ref, ba_ref, tri_ref, o_ref, st_ref):
    @pl.when(pl.program_id(1) == 0)
    def _():
        st_ref[...] = jnp.zeros_like(st_ref)

    tri = tri_ref[...]
    tri_b = tri.astype(BF16)
    causal = tri > 0.0
    wa_b = wa_ref[...].astype(BF16)
    for s in range(q_ref.shape[0]):
        q = q_ref[s].astype(F32) * (GLA_DK ** -0.5)
        k = k_ref[s].astype(F32)
        v = v_ref[s]
        zz = jnp.dot(g_ref[s].astype(BF16), wa_b, preferred_element_type=F32) + ba_ref[...]
        loga = _log_sigmoid(zz) * (1.0 / GLA_TAU)
        bcum = jnp.dot(tri_b, loga.astype(BF16), preferred_element_type=F32)
        btot = _chunk_totals(rev, bcum)
        qd = (q * jnp.exp(bcum)).astype(BF16)
        kd = (k * jnp.exp(-bcum)).astype(BF16)
        kdec = (k * jnp.exp(btot - bcum)).astype(BF16)
        dec = jnp.exp(btot)
        for hh in range(GLA_H):
            sl = slice(hh * LANE, (hh + 1) * LANE)
            qh, vh, kdh = qd[:, sl], v[:, sl], kdec[:, sl]
            att = lax.dot_general(qh, kd[:, sl], _NT, preferred_element_type=F32)
            att = jnp.where(causal, att, 0.0).astype(BF16)
            o_intra = jnp.dot(att, vh, preferred_element_type=F32)
            st = st_ref[s * GLA_H + hh]
            for c in _chunk_order(rev):
                rows = slice(c * CHUNK, (c + 1) * CHUNK)
                o_ref[s, rows, sl] = o_intra[rows] + lax.dot_general(qh[rows], st.astype(BF16), _NT,
                                                                     preferred_element_type=F32)
                st = st * dec[c * CHUNK:c * CHUNK + 1, sl] + lax.dot_general(
                    vh[rows], kdh[rows], _TN, preferred_element_type=F32)
            st_ref[s * GLA_H + hh] = st


def _gla(rev, z, gates, wa, ba, tri):
    b, lt, _ = z.shape
    nb = lt // TB
    d = int(rev)
    blk = lambda i: _scan_block(d, i, nb)
    ns = _mixer_streams(b)
    zspec = lambda col: pl.BlockSpec((ns, TB, HPAD), lambda bi, i: (bi, blk(i), col))
    const = lambda shape: pl.BlockSpec(shape, lambda bi, i: (0,) * len(shape))
    return pl.pallas_call(
        functools.partial(_gla_kernel, rev),
        grid=(b // ns, nb),
        in_specs=[zspec(Z_GQ // HPAD), zspec(Z_GK // HPAD), zspec(Z_GV // HPAD),
                  pl.BlockSpec((ns, TB, LANE), lambda bi, i: (bi, blk(i), 2 * d)),
                  const((LANE, HPAD)), const((1, HPAD)), const((TB, TB))],
        out_specs=pl.BlockSpec((ns, TB, HPAD), lambda bi, i: (bi, blk(i), 0)),
        out_shape=jax.ShapeDtypeStruct((b, lt, HPAD), F32),
        scratch_shapes=[pltpu.VMEM((ns * GLA_H, LANE, LANE), F32)],
        compiler_params=_cparams(("arbitrary", "arbitrary")),
    )(z, z, z, gates, wa[d], ba[d], tri)


def _mlstm_kernel(rev, q_ref, k_ref, v_ref, ga_ref, gb_ref, gat_ref, gbt_ref, ibr_ref, fbr_ref,
                  ibc_ref, fbc_ref, tri_ref, trit_ref, o_ref, ct_ref, m_ref):
    @pl.when(pl.program_id(1) == 0)
    def _():
        ct_ref[...] = jnp.zeros_like(ct_ref)
        m_ref[...] = jnp.zeros_like(m_ref)

    tri = tri_ref[...]
    tri_b = tri.astype(BF16)
    trit_b = trit_ref[...].astype(BF16)
    causal = tri > 0.0
    ones_col = lax.broadcasted_iota(jnp.int32, (TB, LANE), 1) == ML_D
    for s in range(q_ref.shape[0]):
        _mlstm_stream(rev, s, tri_b, trit_b, causal, ones_col, q_ref, k_ref, v_ref, ga_ref, gb_ref,
                      gat_ref, gbt_ref, ibr_ref, fbr_ref, ibc_ref, fbc_ref, o_ref, ct_ref, m_ref)


def _mlstm_stream(rev, s, tri_b, trit_b, causal, ones_col, q_ref, k_ref, v_ref, ga_ref, gb_ref,
                  gat_ref, gbt_ref, ibr_ref, fbr_ref, ibc_ref, fbc_ref, o_ref, ct_ref, m_ref):
    q = q_ref[s]
    k = k_ref[s].astype(F32) * (ML_D ** -0.5)
    v = v_ref[s]
    igc = ga_ref[s] + ibr_ref[...]
    lfc = _log_sigmoid(gb_ref[s] + fbr_ref[...])
    igr = gat_ref[s] + ibc_ref[...]
    lfr = _log_sigmoid(gbt_ref[s] + fbc_ref[...])
    fc = _split_dot(tri_b, lfc, True)
    ftot = _chunk_totals(rev, fc)
    fr = _split_dot(trit_b, lfr, False)
    gc = ftot - fc + igc
    crow = lambda x, c: x[c * CHUNK:c * CHUNK + 1]
    m_loc = [jnp.max(gc[c * CHUNK:(c + 1) * CHUNK], axis=0, keepdims=True) for c in range(NCB)]
    m = m_ref[s]
    m_enter, a_c, b_c = [None] * NCB, [None] * NCB, [None] * NCB
    for c in _chunk_order(rev):
        m_enter[c] = m
        m_new = jnp.maximum(crow(ftot, c) + m, m_loc[c])
        a_c[c] = jnp.exp(crow(ftot, c) + m - m_new)
        b_c[c] = jnp.exp(m_loc[c] - m_new)
        m = m_new
    m_ref[s] = m
    spread = lambda rows_c: jnp.concatenate(
        [jnp.broadcast_to(r, (CHUNK, LANE)) for r in rows_c], axis=0)
    wc = jnp.exp(gc - spread(m_loc))
    inter_all = fc + spread(m_enter)
    for hh in range(ML_H):
        sl = slice(hh * LANE, (hh + 1) * LANE)
        gl = slice(GATE_OFF + hh, GATE_OFF + hh + 1)
        qh = q[:, sl]
        kh = k[:, sl]
        vaug = jnp.where(ones_col, 1.0, v[:, sl].astype(F32)).astype(BF16)
        qk = lax.dot_general(qh, kh.astype(BF16), _NT, preferred_element_type=F32)
        dlog = fc[:, gl] - fr[hh:hh + 1, :] + igr[hh:hh + 1, :]
        dlog = jnp.where(causal, dlog, -jnp.inf)
        inter = inter_all[:, gl]
        m_t = jnp.maximum(inter, jnp.max(dlog, axis=1, keepdims=True))
        w_inter = jnp.exp(inter - m_t)
        p = (jnp.exp(dlog - m_t) * qk).astype(BF16)
        nd_intra = jnp.dot(p, vaug, preferred_element_type=F32)
        floor = jnp.exp(-m_t)
        kw = (kh * wc[:, gl]).astype(BF16)
        ct = ct_ref[s * ML_H + hh]
        for c in _chunk_order(rev):
            rows = slice(c * CHUNK, (c + 1) * CHUNK)
            nd = nd_intra[rows] + w_inter[rows] * lax.dot_general(qh[rows], ct.astype(BF16), _NT,
                                                                  preferred_element_type=F32)
            den = nd[:, ML_D:ML_D + 1]
            o_ref[s, rows, sl] = nd / jnp.maximum(jnp.abs(den), floor[rows])
            ct = a_c[c][:, gl] * ct + b_c[c][:, gl] * lax.dot_general(vaug[rows], kw[rows], _TN,
                                                                      preferred_element_type=F32)
        ct_ref[s * ML_H + hh] = ct


def _mlstm(rev, z, gates, gates_t, ibr, fbr, ibc, fbc, tri, trit):
    b, lt, _ = z.shape
    nb = lt // TB
    d = int(rev)
    blk = lambda i: _scan_block(d, i, nb)
    ns = _mixer_streams(b)
    zspec = lambda col: pl.BlockSpec((ns, TB, HPAD), lambda bi, i: (bi, blk(i), col))
    gspec = lambda t: pl.BlockSpec((ns, TB, LANE), lambda bi, i: (bi, blk(i), 2 * d + t))
    gtspec = lambda t: pl.BlockSpec((ns, 8, TB), lambda bi, i: (bi, 2 * d + t, blk(i)))
    const = lambda shape: pl.BlockSpec(shape, lambda bi, i: (0,) * len(shape))
    return pl.pallas_call(
        functools.partial(_mlstm_kernel, rev),
        grid=(b // ns, nb),
        in_specs=[zspec(Z_MQ // HPAD), zspec(Z_MK // HPAD), zspec(Z_MV // HPAD),
                  gspec(0), gspec(1), gtspec(0), gtspec(1),
                  const((1, LANE)), const((1, LANE)), const((8, 1)), const((8, 1)),
                  const((TB, TB)), const((TB, TB))],
        out_specs=pl.BlockSpec((ns, TB, HPAD), lambda bi, i: (bi, blk(i), 0)),
        out_shape=jax.ShapeDtypeStruct((b, lt, HPAD), F32),
        scratch_shapes=[pltpu.VMEM((ns * ML_H, LANE, LANE), F32), pltpu.VMEM((ns, 1, LANE), F32)],
        compiler_params=_cparams(("arbitrary", "arbitrary")),
    )(z, z, z, gates, gates, gates_t, gates_t, ibr[d], fbr[d], ibc[d], fbc[d], tri, trit)


def _s5_prep_kernel(arc_ref, aic_ref, arr_ref, air_ref, ldt_ref, br_ref, bi_ref, cr_ref, ci_ref,
                    crt_ref, cit_ref, ar_out, ai_out, k_out, k0_out, vr_out, vi_out, l_out):
    nj = S5_T * S5_HC
    jmap = (lax.broadcasted_iota(jnp.int32, (1, nj), 1) // S5_HC).astype(F32)
    fmap = (lax.broadcasted_iota(jnp.int32, (nj, 1), 0) // S5_HC + 1).astype(F32)
    kmats = []
    for d in range(2):
        dt = jnp.exp(ldt_ref[0, d])
        a_re, a_im = arc_ref[0, d], aic_ref[0, d]
        arc, aic = a_re * dt, a_im * dt
        mag = jnp.exp(arc)
        lre, lim = mag * jnp.cos(aic), mag * jnp.sin(aic)
        den = a_re * a_re + a_im * a_im
        z_re = ((lre - 1.0) * a_re + lim * a_im) / den
        z_im = (lim * a_re - (lre - 1.0) * a_im) / den
        b_re, b_im = br_ref[0, d], bi_ref[0, d]
        bb_re = z_re * b_re - z_im * b_im
        bb_im = z_re * b_im + z_im * b_re
        pmag = jnp.exp(arc * jmap)
        p_re, p_im = pmag * jnp.cos(aic * jmap), pmag * jnp.sin(aic * jmap)
        a_st_re = p_re * bb_re - p_im * bb_im
        a_st_im = p_re * bb_im + p_im * bb_re
        ar_out[0, d] = a_st_re
        ai_out[0, d] = a_st_im
        kmat = (jnp.dot(cr_ref[0, d], a_st_re, preferred_element_type=F32, precision=HIGHEST)
                - jnp.dot(ci_ref[0, d], a_st_im, preferred_element_type=F32, precision=HIGHEST))
        k_out[0, d] = kmat
        kmats.append(kmat)
        arr, air = arr_ref[0, d] * dt, air_ref[0, d] * dt
        fmag = jnp.exp(arr * fmap)
        f_re, f_im = fmag * jnp.cos(air * fmap), fmag * jnp.sin(air * fmap)
        c_re, c_im = crt_ref[0, d], cit_ref[0, d]
        vr_out[0, d] = c_re * f_re - c_im * f_im
        vi_out[0, d] = -(c_re * f_im + c_im * f_re)
        tmag = jnp.exp(arr * float(S5_T))
        t_re, t_im = tmag * jnp.cos(air * float(S5_T)), tmag * jnp.sin(air * float(S5_T))
        l_out[0, d, 0:1, 0:S5_P] = t_re
        l_out[0, d, 0:1, S5_P:2 * S5_P] = t_re
        l_out[0, d, 1:2, 0:S5_P] = -t_im
        l_out[0, d, 1:2, S5_P:2 * S5_P] = t_im
    k0_out[0] = kmats[0][:, 0:S5_HC] + kmats[1][:, 0:S5_HC]


def _s5_operators(a_re, a_im, log_dt, b_re, b_im, c_re, c_im):
    g, p, hc, t = S5_G, S5_P, S5_HC, S5_T
    nj = t * hc
    gm = lambda x: jnp.moveaxis(x, 0, 1)
    spec = lambda shape: pl.BlockSpec((1,) + shape, lambda i: (i,) + (0,) * len(shape))
    outs = pl.pallas_call(
        _s5_prep_kernel,
        grid=(g,),
        in_specs=[spec((2, p, 1)), spec((2, p, 1)), spec((2, 1, p)), spec((2, 1, p)), spec((2, 1, 1)),
                  spec((2, p, nj)), spec((2, p, nj)), spec((2, hc, p)), spec((2, hc, p)),
                  spec((2, nj, p)), spec((2, nj, p))],
        out_specs=[spec((2, p, nj)), spec((2, p, nj)), spec((2, hc, nj)), spec((hc, hc)),
                   spec((2, nj, p)), spec((2, nj, p)), spec((2, 2, 2 * p))],
        out_shape=[jax.ShapeDtypeStruct((g, 2, p, nj), F32), jax.ShapeDtypeStruct((g, 2, p, nj), F32),
                   jax.ShapeDtypeStruct((g, 2, hc, nj), F32), jax.ShapeDtypeStruct((g, hc, hc), F32),
                   jax.ShapeDtypeStruct((g, 2, nj, p), F32), jax.ShapeDtypeStruct((g, 2, nj, p), F32),
                   jax.ShapeDtypeStruct((g, 2, 2, 2 * p), F32)],
        compiler_params=_cparams(("arbitrary",)),
    )(gm(a_re)[..., None], gm(a_im)[..., None], gm(a_re)[:, :, None, :], gm(a_im)[:, :, None, :],
      gm(log_dt)[..., None, None],
      jnp.tile(gm(b_re), (1, 1, 1, t)), jnp.tile(gm(b_im), (1, 1, 1, t)), gm(c_re), gm(c_im),
      jnp.tile(gm(c_re), (1, 1, t, 1)), jnp.tile(gm(c_im), (1, 1, t, 1)))
    a_st_re, a_st_im, kmat, k0, v_re, v_im, lam_t = outs

    a5 = jnp.stack([a_st_re, a_st_im], axis=2).reshape(g, 2, 2, p, t, hc)
    a5 = jnp.stack([jnp.flip(a5[:, 0], axis=3), a5[:, 1]], axis=1)
    wcat = jnp.transpose(a5, (0, 4, 5, 1, 2, 3)).reshape(g, nj, 4 * p)
    v5 = jnp.stack([v_re, v_im], axis=2).reshape(g, 2, 2, t, hc, p)
    v5 = jnp.stack([v5[:, 0], jnp.flip(v5[:, 1], axis=2)], axis=1)
    vcat = jnp.transpose(v5, (0, 1, 2, 5, 3, 4)).reshape(g, 4 * p, nj)
    k5 = kmat.reshape(g, 2, hc, t, hc)
    ss = np.arange(t)[:, None]
    tt = np.arange(t)[None, :]
    lag = np.abs(tt - ss)
    kf = k5[:, 0][:, :, lag, :]
    kb = k5[:, 1][:, :, lag, :]
    sel_f = jnp.asarray(tt > ss)[None, None, :, :, None]
    sel_b = jnp.asarray(ss > tt)[None, None, :, :, None]
    m5 = jnp.where(sel_f, kf, jnp.where(sel_b, kb, k0[:, :, None, None, :]))
    mfb = jnp.transpose(m5, (0, 2, 4, 3, 1)).reshape(g, nj, nj)
    return wcat.astype(BF16), vcat.astype(BF16), mfb.astype(BF16), lam_t


S5_CB = TB // S5_T
S5_GH = LANE // S5_HC


def _s5_group_rows(ua_ref, ub_ref, ug_ref):
    nbat = ua_ref.shape[0]
    for t in range(S5_T):
        for half, ref in enumerate((ua_ref, ub_ref)):
            rows = jnp.concatenate([ref[bi, pl.ds(t, S5_CB, stride=S5_T), :] for bi in range(nbat)], axis=0)
            for gg in range(S5_GH):
                ug_ref[half * S5_GH + gg, :, t * S5_HC:(t + 1) * S5_HC] = rows[:, gg * S5_HC:(gg + 1) * S5_HC]


def _s5_dx_kernel(ua_ref, ub_ref, w_ref, dxf_ref, dxb_ref, ug_ref):
    nbat = ua_ref.shape[0]
    _s5_group_rows(ua_ref, ub_ref, ug_ref)
    for g in range(S5_G):
        res = jnp.dot(ug_ref[g].astype(BF16), w_ref[g], preferred_element_type=F32)
        for bi in range(nbat):
            rows = slice(bi * S5_CB, (bi + 1) * S5_CB)
            dxf_ref[bi, pl.ds(g, S5_CB, stride=S5_G), :] = res[rows, 0:LANE]
            dxb_ref[bi, pl.ds(g, S5_CB, stride=S5_G), :] = res[rows, LANE:2 * LANE]


def _s5_scan_kernel(dxf_ref, dxb_ref, l_ref, xef_ref, xeb_ref, sf_ref, sb_ref):
    nbat = dxf_ref.shape[0]

    @pl.when(pl.program_id(0) == 0)
    def _():
        sf_ref[...] = jnp.zeros_like(sf_ref)
        sb_ref[...] = jnp.zeros_like(sb_ref)

    tile = lambda r: jnp.concatenate([l_ref[r]] * nbat, axis=0)
    la_f, lb_f, la_b, lb_b = tile(0), tile(1), tile(2), tile(3)

    def step(j, carry):
        xf, xb = carry
        rf = pl.ds(pl.multiple_of(j * S5_G, S5_G), S5_G)
        rb = pl.ds(pl.multiple_of((S5_CB - 1 - j) * S5_G, S5_G), S5_G)
        xef_ref[:, rf, :] = xf.reshape(nbat, S5_G, LANE)
        xeb_ref[:, rb, :] = xb.reshape(nbat, S5_G, LANE)
        dxf = dxf_ref[:, rf, :].reshape(nbat * S5_G, LANE)
        dxb = dxb_ref[:, rb, :].reshape(nbat * S5_G, LANE)
        xf = la_f * xf + lb_f * pltpu.roll(xf, S5_P, 1) + dxf
        xb = la_b * xb + lb_b * pltpu.roll(xb, S5_P, 1) + dxb
        return xf, xb

    xf, xb = lax.fori_loop(0, S5_CB, step, (sf_ref[...], sb_ref[...]))
    sf_ref[...] = xf
    sb_ref[...] = xb


def _s5_y_kernel(ua_ref, ub_ref, xef_ref, xeb_ref, m_ref, v_ref, ya_ref, yb_ref, ug_ref, yg_ref):
    nbat = ua_ref.shape[0]
    _s5_group_rows(ua_ref, ub_ref, ug_ref)
    for g in range(S5_G):
        grows = lambda ref: jnp.concatenate(
            [ref[bi, pl.ds(g, S5_CB, stride=S5_G), :] for bi in range(nbat)], axis=0)
        xe = jnp.concatenate([grows(xef_ref), grows(xeb_ref)], axis=1).astype(BF16)
        yg_ref[g] = (jnp.dot(ug_ref[g].astype(BF16), m_ref[g], preferred_element_type=F32)
                     + jnp.dot(xe, v_ref[g], preferred_element_type=F32))
    for t in range(S5_T):
        for half, ref in enumerate((ya_ref, yb_ref)):
            rows = jnp.concatenate([yg_ref[half * S5_GH + gg, :, t * S5_HC:(t + 1) * S5_HC]
                                    for gg in range(S5_GH)], axis=1)
            for bi in range(nbat):
                ref[bi, pl.ds(t, S5_CB, stride=S5_T), :] = rows[bi * S5_CB:(bi + 1) * S5_CB]


def _s5(ua, ub, wcat, vcat, mfb, lam_t):
    b, lt, _ = ua.shape
    g, nj = S5_G, S5_T * S5_HC
    nb = lt // TB
    srows = S5_CB * g
    tok = pl.BlockSpec((b, TB, LANE), lambda i: (0, i, 0))
    full = lambda a: pl.BlockSpec(a.shape, lambda i: (0,) * a.ndim)
    st_f = pl.BlockSpec((b, srows, LANE), lambda i: (0, i, 0))
    st_b = pl.BlockSpec((b, srows, LANE), lambda i: (0, _scan_block(1, i, nb), 0))
    st_shape = jax.ShapeDtypeStruct((b, (lt // S5_T) * g, LANE), F32)
    dxf, dxb = pl.pallas_call(
        _s5_dx_kernel,
        grid=(nb,),
        in_specs=[tok, tok, full(wcat)],
        out_specs=[st_f, st_f],
        out_shape=[st_shape, st_shape],
        scratch_shapes=[pltpu.VMEM((g, b * S5_CB, nj), F32)],
        compiler_params=_cparams(("arbitrary",)),
    )(ua, ub, wcat)
    lam4 = jnp.moveaxis(lam_t.reshape(g, 4, LANE), 1, 0)
    xef, xeb = pl.pallas_call(
        _s5_scan_kernel,
        grid=(nb,),
        in_specs=[st_f, st_b, full(lam4)],
        out_specs=[st_f, st_b],
        out_shape=[st_shape, st_shape],
        scratch_shapes=[pltpu.VMEM((b * g, LANE), F32)] * 2,
        compiler_params=_cparams(("arbitrary",)),
    )(dxf, dxb, lam4)
    return pl.pallas_call(
        _s5_y_kernel,
        grid=(nb,),
        in_specs=[tok, tok, st_f, st_f, full(mfb), full(vcat)],
        out_specs=[tok, tok],
        out_shape=[jax.ShapeDtypeStruct((b, lt, LANE), F32)] * 2,
        scratch_shapes=[pltpu.VMEM((g, b * S5_CB, nj), F32)] * 2,
        compiler_params=_cparams(("arbitrary",)),
    )(ua, ub, xef, xeb, mfb, vcat)


def _head_ln(o, gain):
    valid = lax.broadcasted_iota(jnp.int32, (1, LANE), 1) < GLA_DV
    outs = []
    for hh in range(GLA_H):
        oh = o[:, hh * LANE:(hh + 1) * LANE]
        mu = jnp.sum(jnp.where(valid, oh, 0.0), axis=-1, keepdims=True) * (1.0 / GLA_DV)
        oc = jnp.where(valid, oh - mu, 0.0)
        var = jnp.sum(oc * oc, axis=-1, keepdims=True) * (1.0 / GLA_DV)
        outs.append(oc * lax.rsqrt(var + LN_EPS))
    return jnp.concatenate(outs, axis=-1) * gain


def _out_kernel(alpha, hc_ref, hx_ref, mod_ref, ya_ref, yb_ref, ua_ref, ub_ref, gr_ref, mo_ref, gf_ref, gb_ref,
                mf_ref, mb_ref, dsk_ref, wglu_ref, bglu_ref, gg_ref, mg_ref, wout_ref, l1g_ref, l1b_ref,
                wup_ref, h1_ref, a_ref, v_ref, mix_ref):
    mod = mod_ref[0, 0]
    ys = jnp.concatenate([ya_ref[0], yb_ref[0]], axis=1)
    u = jnp.concatenate([ua_ref[0], ub_ref[0]], axis=1)
    y1 = jax.nn.gelu(ys + dsk_ref[...] * u)
    glu = jnp.dot(y1.astype(BF16), wglu_ref[...], preferred_element_type=F32) + bglu_ref[...]
    mix_ref[:, 0:S5_W] = (y1 * jax.nn.sigmoid(glu)).astype(BF16)
    r = gr_ref[0].astype(F32)
    glo = _head_ln(gf_ref[0] + gb_ref[0], gg_ref[...])
    mix_ref[:, S5_W:S5_W + HPAD] = (r * jax.nn.sigmoid(r) * glo).astype(BF16)
    mlo = _head_ln(mf_ref[0] + mb_ref[0], mg_ref[...])
    mix_ref[:, S5_W + HPAD:MIX_W] = (jax.nn.sigmoid(mo_ref[0].astype(F32)) * mlo).astype(BF16)
    mixed = jnp.dot(mix_ref[...], wout_ref[...], preferred_element_type=F32)
    y = alpha * _stream_block(hc_ref, hx_ref) + mod[2:3] * mixed
    h1 = _ln_rows(y) * l1g_ref[...] + l1b_ref[...]
    h1_ref[0] = h1
    hm = (_ln_rows(h1) * (1.0 + mod[4:5]) + mod[3:4]).astype(BF16)
    a_ref[0] = jnp.dot(hm, wup_ref[:, 0:D_FF], preferred_element_type=F32).astype(BF16)
    v_ref[0] = jnp.dot(hm, wup_ref[:, D_FF:2 * D_FF], preferred_element_type=F32).astype(BF16)


def _out_proj(alpha, h, modtab, ys5, us5, z, ogla, oml, dsk, wglu, bglu, gg, mg, wout, l1g, l1b, wup):
    h_ops, h_specs, (b, lt) = _stream_operands(h)
    d = D_MODEL
    nb = lt // TB
    const = lambda shape: pl.BlockSpec(shape, lambda bi, j: (0,) * len(shape))
    tok = lambda w, col=0: pl.BlockSpec((1, TB, w), lambda bi, j: (bi, j, col))
    return pl.pallas_call(
        functools.partial(_out_kernel, alpha),
        grid=(b, nb),
        in_specs=h_specs + [
                  pl.BlockSpec((1, 1, 6, d), lambda bi, j: (bi, jnp.minimum(j, 1), 0, 0)),
                  tok(LANE), tok(LANE), tok(LANE), tok(LANE), tok(HPAD, Z_GR // HPAD), tok(HPAD, Z_MO // HPAD),
                  tok(HPAD), tok(HPAD), tok(HPAD), tok(HPAD),
                  const((1, S5_W)), const((S5_W, S5_W)), const((1, S5_W)), const((1, HPAD)), const((1, HPAD)),
                  const((MIX_W, d)), const((1, d)), const((1, d)), const((d, 2 * D_FF))],
        out_specs=[tok(d), tok(D_FF), tok(D_FF)],
        out_shape=[jax.ShapeDtypeStruct((b, lt, d), F32), jax.ShapeDtypeStruct((b, lt, D_FF), BF16),
                   jax.ShapeDtypeStruct((b, lt, D_FF), BF16)],
        scratch_shapes=[pltpu.VMEM((TB, MIX_W), BF16)],
        compiler_params=_cparams(("arbitrary", "arbitrary")),
    )(*h_ops, modtab, ys5[0], ys5[1], us5[0], us5[1], z, z, ogla[0], ogla[1], oml[0], oml[1], dsk, wglu, bglu,
      gg, mg, wout, l1g, l1b, wup)


FF_CW = 256
GELU_E0 = -2.0 * math.sqrt(2.0 / math.pi) * math.log2(math.e)
GELU_E1 = 0.044715 * GELU_E0


def _ffn_kernel(alpha, nb, h_ref, mod_ref, a_ref, ap_ref, an_ref, v_ref, wc_ref, bc_ref, wd_ref,
                l2g_ref, l2b_ref, o_ref, g_ref):
    j = pl.program_id(1)
    mod = mod_ref[0, 0]
    is_lat = j > 0
    t_idx = lax.broadcasted_iota(jnp.int32, (TB, 1), 0)
    col = jnp.where(is_lat, t_idx % GRID_W, t_idx)
    last = jnp.where(is_lat, GRID_W - 1, TB - 1)
    has_l = jnp.broadcast_to((col > 0).astype(F32), (TB, FF_CW))
    has_r = jnp.broadcast_to((col < last).astype(F32), (TB, FF_CW))
    vert = is_lat.astype(F32)
    up_ok = (j > 1).astype(F32)
    dn_ok = (j < nb - 1).astype(F32)
    for c0 in range(0, D_FF, FF_CW):
        cs = slice(c0, c0 + FF_CW)
        a = a_ref[0, :, cs].astype(F32)
        prev = ap_ref[0, :, cs].astype(F32) * up_ok
        nxt = an_ref[0, :, cs].astype(F32) * dn_ok
        up = jnp.concatenate([prev, a[:TB - GRID_W]], axis=0)
        dn = jnp.concatenate([a[GRID_W:], nxt], axis=0)
        w = wc_ref[:, cs]
        wv = w * vert
        left = up * wv[0:1] + a * w[3:4] + dn * wv[6:7]
        mid = up * wv[1:2] + a * w[4:5] + dn * wv[7:8]
        right = up * wv[2:3] + a * w[5:6] + dn * wv[8:9]
        acc = (mid + bc_ref[:, cs]) + has_l * pltpu.roll(left, 1, 0) + has_r * pltpu.roll(right, TB - 1, 0)
        e = jnp.exp2(acc * (GELU_E0 + GELU_E1 * (acc * acc)))
        g_ref[:, cs] = ((acc * v_ref[0, :, cs].astype(F32)) / (1.0 + e)).astype(BF16)
    f = jnp.dot(g_ref[...], wd_ref[...], preferred_element_type=F32)
    y = alpha * h_ref[0] + mod[5:6] * f
    o_ref[0] = _ln_rows(y) * l2g_ref[...] + l2b_ref[...]


def _ffn(alpha, h1, modtab, a, v, wc, bc, wd, l2g, l2b, latent_only):
    b, lt, d = h1.shape
    nb = lt // TB
    rpb = TB // GRID_W
    nrow = lt // GRID_W
    const = lambda shape: pl.BlockSpec(shape, lambda bi, j: (0,) * len(shape))
    tok = lambda w: pl.BlockSpec((1, TB, w), lambda bi, j: (bi, j, 0))
    if latent_only:
        out_spec = pl.BlockSpec((1, TB, d), lambda bi, j: (bi, jnp.maximum(j - 1, 0), 0))
        out_len = lt - TB
    else:
        out_spec, out_len = tok(d), lt
    return pl.pallas_call(
        functools.partial(_ffn_kernel, alpha, nb),
        grid=(b, nb),
        in_specs=[tok(d), pl.BlockSpec((1, 1, 6, d), lambda bi, j: (bi, jnp.minimum(j, 1), 0, 0)),
                  tok(D_FF),
                  pl.BlockSpec((1, GRID_W, D_FF), lambda bi, j: (bi, jnp.maximum(j * rpb - 1, 0), 0)),
                  pl.BlockSpec((1, GRID_W, D_FF), lambda bi, j: (bi, jnp.minimum((j + 1) * rpb, nrow - 1), 0)),
                  tok(D_FF), const((9, D_FF)), const((1, D_FF)), const((D_FF, d)),
                  const((1, d)), const((1, d))],
        out_specs=out_spec,
        out_shape=jax.ShapeDtypeStruct((b, out_len, d), F32),
        scratch_shapes=[pltpu.VMEM((TB, D_FF), BF16)],
        compiler_params=_cparams(("arbitrary", "arbitrary")),
    )(h1, modtab, a, a, a, v, wc, bc, wd, l2g, l2b)


def _pad_heads(w, nh, dh):
    lead = w.shape[:-1]
    w = w.reshape(lead + (nh, dh))
    w = jnp.pad(w, [(0, 0)] * len(lead) + [(0, 0), (0, LANE - dh)])
    return w.reshape(lead + (nh * LANE,))


def _pad_head_rows(w, nh, dh):
    return jnp.moveaxis(_pad_heads(jnp.moveaxis(w, 0, -1), nh, dh), -1, 0)


def _layer_weights(w_in, gla_w_a2, gla_b_a, ml_i_bias, ml_f_bias, w_out):
    d = w_in.shape[0]
    offs = np.cumsum((0, S5_W, GLA_H * GLA_DK, GLA_H * GLA_DK, GLA_W, GLA_W, 2 * GLA_RANK,
                      ML_W, ML_W, ML_W, ML_W, 2 * ML_H, 2 * ML_H))
    part = lambda i: w_in[:, offs[i]:offs[i + 1]]
    w_p = jnp.concatenate([
        _pad_heads(part(1), GLA_H, GLA_DK), _pad_heads(part(2), GLA_H, GLA_DK),
        _pad_heads(part(3), GLA_H, GLA_DV), _pad_heads(part(4), GLA_H, GLA_DV),
        _pad_heads(part(6), ML_H, ML_D), _pad_heads(part(7), ML_H, ML_D),
        _pad_heads(part(8), ML_H, ML_D), _pad_heads(part(9), ML_H, ML_D)], axis=1).astype(BF16)
    w_u = part(0).astype(BF16)
    lr, ig, fg = part(5), part(10), part(11)
    zeros = lambda n: jnp.zeros((d, n), w_in.dtype)
    tiles = []
    for dd in range(2):
        tiles += [lr[:, dd * GLA_RANK:(dd + 1) * GLA_RANK], ig[:, dd * ML_H:(dd + 1) * ML_H],
                  zeros(LANE - GLA_RANK - ML_H),
                  zeros(GATE_OFF), fg[:, dd * ML_H:(dd + 1) * ML_H], zeros(LANE - GATE_OFF - ML_H)]
    w_g = jnp.concatenate(tiles, axis=1).astype(BF16)
    rows_t = []
    for dd in range(2):
        for gate in (ig, fg):
            rows_t += [gate[:, dd * ML_H:(dd + 1) * ML_H], zeros(8 - ML_H)]
    w_gt = jnp.concatenate(rows_t, axis=1).T.astype(BF16)
    wa = jnp.pad(_pad_heads(gla_w_a2, GLA_H, GLA_DK), ((0, 0), (0, LANE - GLA_RANK), (0, 0)))
    ba = _pad_heads(gla_b_a, GLA_H, GLA_DK)[:, None, :]
    gate_row = lambda bias: jnp.pad(bias, ((0, 0), (GATE_OFF, LANE - GATE_OFF - ML_H)))[:, None, :]
    gate_col = lambda bias: jnp.pad(bias, ((0, 0), (0, 8 - ML_H)))[:, :, None]
    wout_p = jnp.concatenate([
        w_out[:S5_W], _pad_head_rows(w_out[S5_W:S5_W + GLA_W], GLA_H, GLA_DV),
        _pad_head_rows(w_out[S5_W + GLA_W:], ML_H, ML_D)], axis=0).astype(BF16)
    return dict(w_p=w_p, w_u=w_u, w_g=w_g, w_gt=w_gt, wa=wa, ba=ba,
                ibr=gate_row(ml_i_bias), fbr=gate_row(ml_f_bias),
                ibc=gate_col(ml_i_bias), fbc=gate_col(ml_f_bias), wout=wout_p)


def kernel(x, c, ctx, c_ctx, w_ada, b_ada, w_in, s5_a_re, s5_a_im, s5_log_dt, s5_b_re, s5_b_im, s5_c_re, s5_c_im, s5_d, s5_w_glu, s5_b_glu, gla_w_a2, gla_b_a, gla_g, ml_i_bias, ml_f_bias, ml_g, w_out, ln1_g, ln1_b, w_up, w_dconv, b_dconv, w_down, ln2_g, ln2_b):
    b, seq, d = x.shape
    ctx_len = ctx.shape[1]
    depth = w_in.shape[0]
    assert d == D_MODEL and ctx_len == TB and seq % TB == 0 and b <= 7
    alpha = (2.0 * depth) ** 0.25

    cc = jnp.zeros((8, d), F32).at[:b].set(c).at[b].set(c_ctx)
    mod = _modulation(cc, w_ada, b_ada).reshape(depth, 8, 6, d)
    modtab = jnp.stack([jnp.broadcast_to(mod[:, b:b + 1], (depth, b, 6, d)), mod[:, :b]], axis=2)

    eye = np.eye(NCB, dtype=np.float32)
    low = np.kron(eye, np.tril(np.ones((CHUNK, CHUNK), np.float32)))
    tri = (jnp.asarray(low), jnp.asarray(low.T))

    h = (ctx, x)
    for l in range(depth):
        lw = _layer_weights(w_in[l], gla_w_a2[l], gla_b_a[l], ml_i_bias[l], ml_f_bias[l], w_out[l])
        z, ua, ub, gates, gates_t = _in_proj(h, modtab[l], lw["w_p"], lw["w_u"], lw["w_g"], lw["w_gt"])
        ogla = [_gla(rev, z, gates, lw["wa"], lw["ba"], tri[rev]) for rev in (0, 1)]
        oml = [_mlstm(rev, z, gates, gates_t, lw["ibr"], lw["fbr"], lw["ibc"], lw["fbc"],
                      tri[rev], tri[1 - rev]) for rev in (0, 1)]
        wcat, vcat, mfb, lam_t = _s5_operators(s5_a_re[l], s5_a_im[l], s5_log_dt[l], s5_b_re[l],
                                               s5_b_im[l], s5_c_re[l], s5_c_im[l])
        ys5 = _s5(ua, ub, wcat, vcat, mfb, lam_t)
        h1, a, v = _out_proj(alpha, h, modtab[l], ys5, (ua, ub), z, ogla, oml, s5_d[l][None],
                             s5_w_glu[l].astype(BF16),
                             s5_b_glu[l][None], _pad_heads(gla_g[l], GLA_H, GLA_DV)[None],
                             _pad_heads(ml_g[l], ML_H, ML_D)[None], lw["wout"], ln1_g[l][None],
                             ln1_b[l][None], w_up[l].astype(BF16))
        h = _ffn(alpha, h1, modtab[l], a, v, w_dconv[l].reshape(9, D_FF), b_dconv[l][None],
                 w_down[l].astype(BF16), ln2_g[l][None], ln2_b[l][None], latent_only=(l == depth - 1))
    return h
```

```python
import functools
import math

import numpy as np
import jax
import jax.numpy as jnp
from jax import lax
from jax.experimental import pallas as pl
from jax.experimental.pallas import tpu as pltpu

F32 = jnp.float32
BF16 = jnp.bfloat16
HIGHEST = lax.Precision.HIGHEST

D_MODEL = 1024
GRID_W = 64
CHUNK = 64
S5_W = D_MODEL // 4
S5_HC = 16
S5_G = S5_W // S5_HC
S5_P = 64
S5_T = 16
GLA_W = 3 * D_MODEL // 8
GLA_H = 4
GLA_DV = GLA_W // GLA_H
GLA_DK = GLA_DV // 2
GLA_RANK = 16
GLA_TAU = 16.0
ML_W = 3 * D_MODEL // 8
ML_H = 4
ML_D = ML_W // ML_H
D_FF = ((8 * D_MODEL // 3 + 127) // 128) * 128
LN_EPS = 1e-5

LANE = 128
HPAD = GLA_H * LANE
TB = 256
NCB = TB // CHUNK
Z_GQ, Z_GK, Z_GV, Z_GR = 0, HPAD, 2 * HPAD, 3 * HPAD
Z_MQ, Z_MK, Z_MV, Z_MO = 4 * HPAD, 5 * HPAD, 6 * HPAD, 7 * HPAD
NZM = 8 * HPAD
NGATE = 4 * LANE
NGATE_T = 32
GATE_OFF = 16
MIX_W = S5_W + 2 * HPAD

VMEM_LIMIT = 56 * 1024 * 1024


def _cparams(sem):
    return pltpu.CompilerParams(dimension_semantics=sem, vmem_limit_bytes=VMEM_LIMIT)


def _ln_rows(x):
    mu = jnp.mean(x, axis=-1, keepdims=True)
    xc = x - mu
    var = jnp.mean(xc * xc, axis=-1, keepdims=True)
    return xc * lax.rsqrt(var + LN_EPS)


def _log_sigmoid(x):
    return jnp.minimum(x, 0.0) - jnp.log(1.0 + jnp.exp(-jnp.abs(x)))


def _split_dot(tri_b, x, left):
    hi = x.astype(BF16)
    lo = (x - hi.astype(F32)).astype(BF16)
    if left:
        return (jnp.dot(tri_b, hi, preferred_element_type=F32)
                + jnp.dot(tri_b, lo, preferred_element_type=F32))
    return (jnp.dot(hi, tri_b, preferred_element_type=F32)
            + jnp.dot(lo, tri_b, preferred_element_type=F32))


_NT = (((1,), (1,)), ((), ()))
_TN = (((0,), (0,)), ((), ()))


def _mod_kernel(c_ref, w_ref, b_ref, o_ref):
    cc = c_ref[...]
    s = cc * jax.nn.sigmoid(cc)
    o_ref[0] = jnp.dot(s, w_ref[0], preferred_element_type=F32, precision=HIGHEST) + b_ref[0]


def _modulation(cc, w_ada, b_ada):
    depth, d, n = w_ada.shape
    nb = 4
    bn = n // nb
    return pl.pallas_call(
        _mod_kernel,
        grid=(depth, nb),
        in_specs=[pl.BlockSpec((8, d), lambda l, j: (0, 0)),
                  pl.BlockSpec((1, d, bn), lambda l, j: (l, 0, j)),
                  pl.BlockSpec((1, 1, bn), lambda l, j: (l, 0, j))],
        out_specs=pl.BlockSpec((1, 8, bn), lambda l, j: (l, 0, j)),
        out_shape=jax.ShapeDtypeStruct((depth, 8, n), F32),
        compiler_params=_cparams(("arbitrary", "arbitrary")),
    )(cc, w_ada, b_ada.reshape(depth, 1, n))


def _stream_operands(h, rows=1):
    if isinstance(h, tuple):
        ctx, x = h
        total = ctx.shape[1] + x.shape[1]
        run = pl.BlockSpec((rows, TB, x.shape[2]), lambda bi, j: (bi, jnp.maximum(j - 1, 0), 0))
    else:
        ctx = x = h
        total = x.shape[1]
        run = pl.BlockSpec((rows, TB, x.shape[2]), lambda bi, j: (bi, j, 0))
    head = pl.BlockSpec((rows, TB, x.shape[2]), lambda bi, j: (bi, 0, 0))
    return (ctx, x), [head, run], (x.shape[0], total)


def _stream_block(hc_ref, hx_ref, r=0):
    return jnp.where(pl.program_id(1) == 0, hc_ref[r], hx_ref[r])


IN_ROWS = 2


def _in_kernel(hc_ref, hx_ref, mod_ref, w_ref, wu_ref, wg_ref, wgt_ref, z_ref, ua_ref, ub_ref, g_ref, gt_ref):
    hms = []
    for r in range(hx_ref.shape[0]):
        mod = mod_ref[r, 0]
        hms.append((_ln_rows(_stream_block(hc_ref, hx_ref, r)) * (1.0 + mod[1:2]) + mod[0:1]).astype(BF16))
    for r, hm in enumerate(hms):
        for c0 in range(0, NZM, HPAD):
            z_ref[r, :, c0:c0 + HPAD] = jnp.dot(hm, w_ref[:, c0:c0 + HPAD],
                                                preferred_element_type=F32).astype(BF16)
        u = jnp.dot(hm, wu_ref[...], preferred_element_type=F32)
        ua_ref[r] = u[:, :LANE]
        ub_ref[r] = u[:, LANE:]
        g_ref[r] = jnp.dot(hm, wg_ref[...], preferred_element_type=F32)
        gt_ref[r] = lax.dot_general(wgt_ref[...], hm, _NT, preferred_element_type=F32)


def _in_proj(h, modtab, w_p, w_u, w_g, w_gt):
    d = D_MODEL
    rows = IN_ROWS if modtab.shape[0] % IN_ROWS == 0 else 1
    h_ops, h_specs, (b, lt) = _stream_operands(h, rows)
    nb = lt // TB
    const = lambda shape: pl.BlockSpec(shape, lambda bi, j: (0,) * len(shape))
    tok = lambda w: pl.BlockSpec((rows, TB, w), lambda bi, j: (bi, j, 0))
    return pl.pallas_call(
        _in_kernel,
        grid=(b // rows, nb),
        in_specs=h_specs + [
                  pl.BlockSpec((rows, 1, 6, d), lambda bi, j: (bi, jnp.minimum(j, 1), 0, 0)),
                  const((d, NZM)), const((d, S5_W)), const((d, NGATE)), const((NGATE_T, d))],
        out_specs=[tok(NZM), tok(LANE), tok(LANE), tok(NGATE),
                   pl.BlockSpec((rows, NGATE_T, TB), lambda bi, j: (bi, 0, j))],
        out_shape=[jax.ShapeDtypeStruct((b, lt, NZM), BF16),
                   jax.ShapeDtypeStruct((b, lt, LANE), F32),
                   jax.ShapeDtypeStruct((b, lt, LANE), F32),
                   jax.ShapeDtypeStruct((b, lt, NGATE), F32),
                   jax.ShapeDtypeStruct((b, NGATE_T, lt), F32)],
        compiler_params=_cparams(("arbitrary", "arbitrary")),
    )(*h_ops, modtab, w_p, w_u, w_g, w_gt)


def _scan_block(d, i, nb):
    return jnp.where(d == 0, i, jnp.where(i == 0, 0, nb - i))


MIXER_STREAMS = 4


def _mixer_streams(b):
    return MIXER_STREAMS if b % MIXER_STREAMS == 0 else 1


def _chunk_order(rev):
    return range(NCB - 1, -1, -1) if rev else range(NCB)


def _chunk_totals(rev, cum):
    last = 0 if rev else CHUNK - 1
    return jnp.concatenate(
        [jnp.broadcast_to(cum[c * CHUNK + last:c * CHUNK + last + 1], (CHUNK, cum.shape[1]))
         for c in range(NCB)], axis=0)


def _gla_kernel(rev, q_ref, k_ref, v_ref, g_ref, wa_ref, ba_ref, tri_ref, o_ref, st_ref):
    @pl.when(pl.program_id(1) == 0)
    def _():
        st_ref[...] = jnp.zeros_like(st_ref)

    tri = tri_ref[...]
    tri_b = tri.astype(BF16)
    causal = tri > 0.0
    wa_b = wa_ref[...].astype(BF16)
    for s in range(q_ref.shape[0]):
        q = q_ref[s].astype(F32) * (GLA_DK ** -0.5)
        k = k_ref[s].astype(F32)
        v = v_ref[s]
        zz = jnp.dot(g_ref[s].astype(BF16), wa_b, preferred_element_type=F32) + ba_ref[...]
        loga = _log_sigmoid(zz) * (1.0 / GLA_TAU)
        bcum = jnp.dot(tri_b, loga.astype(BF16), preferred_element_type=F32)
        btot = _chunk_totals(rev, bcum)
        qd = (q * jnp.exp(bcum)).astype(BF16)
        kd = (k * jnp.exp(-bcum)).astype(BF16)
        kdec = (k * jnp.exp(btot - bcum)).astype(BF16)
        dec = jnp.exp(btot)
        for hh in range(GLA_H):
            sl = slice(hh * LANE, (hh + 1) * LANE)
            qh, vh, kdh = qd[:, sl], v[:, sl], kdec[:, sl]
            att = lax.dot_general(qh, kd[:, sl], _NT, preferred_element_type=F32)
            att = jnp.where(causal, att, 0.0).astype(BF16)
            o_intra = jnp.dot(att, vh, preferred_element_type=F32)
            st = st_ref[s * GLA_H + hh]
            for c in _chunk_order(rev):
                rows = slice(c * CHUNK, (c + 1) * CHUNK)
                o_ref[s, rows, sl] = o_intra[rows] + lax.dot_general(qh[rows], st.astype(BF16), _NT,
                                                                     preferred_element_type=F32)
                st = st * dec[c * CHUNK:c * CHUNK + 1, sl] + lax.dot_general(
                    vh[rows], kdh[rows], _TN, preferred_element_type=F32)
            st_ref[s * GLA_H + hh] = st


def _gla(rev, z, gates, wa, ba, tri):
    b, lt, _ = z.shape
    nb = lt // TB
    d = int(rev)
    blk = lambda i: _scan_block(d, i, nb)
    ns = _mixer_streams(b)
    zspec = lambda col: pl.BlockSpec((ns, TB, HPAD), lambda bi, i: (bi, blk(i), col))
    const = lambda shape: pl.BlockSpec(shape, lambda bi, i: (0,) * len(shape))
    return pl.pallas_call(
        functools.partial(_gla_kernel, rev),
        grid=(b // ns, nb),
        in_specs=[zspec(Z_GQ // HPAD), zspec(Z_GK // HPAD), zspec(Z_GV // HPAD),
                  pl.BlockSpec((ns, TB, LANE), lambda bi, i: (bi, blk(i), 2 * d)),
                  const((LANE, HPAD)), const((1, HPAD)), const((TB, TB))],
        out_specs=pl.BlockSpec((ns, TB, HPAD), lambda bi, i: (bi, blk(i), 0)),
        out_shape=jax.ShapeDtypeStruct((b, lt, HPAD), F32),
        scratch_shapes=[pltpu.VMEM((ns * GLA_H, LANE, LANE), F32)],
        compiler_params=_cparams(("arbitrary", "arbitrary")),
    )(z, z, z, gates, wa[d], ba[d], tri)


def _mlstm_kernel(rev, q_ref, k_ref, v_ref, ga_ref, gb_ref, gat_ref, gbt_ref, ibr_ref, fbr_ref,
                  ibc_ref, fbc_ref, tri_ref, trit_ref, o_ref, ct_ref, m_ref):
    @pl.when(pl.program_id(1) == 0)
    def _():
        ct_ref[...] = jnp.zeros_like(ct_ref)
        m_ref[...] = jnp.zeros_like(m_ref)

    tri = tri_ref[...]
    tri_b = tri.astype(BF16)
    trit_b = trit_ref[...].astype(BF16)
    causal = tri > 0.0
    ones_col = lax.broadcasted_iota(jnp.int32, (TB, LANE), 1) == ML_D
    for s in range(q_ref.shape[0]):
        _mlstm_stream(rev, s, tri_b, trit_b, causal, ones_col, q_ref, k_ref, v_ref, ga_ref, gb_ref,
                      gat_ref, gbt_ref, ibr_ref, fbr_ref, ibc_ref, fbc_ref, o_ref, ct_ref, m_ref)


def _mlstm_stream(rev, s, tri_b, trit_b, causal, ones_col, q_ref, k_ref, v_ref, ga_ref, gb_ref,
                  gat_ref, gbt_ref, ibr_ref, fbr_ref, ibc_ref, fbc_ref, o_ref, ct_ref, m_ref):
    q = q_ref[s]
    k = k_ref[s].astype(F32) * (ML_D ** -0.5)
    v = v_ref[s]
    igc = ga_ref[s] + ibr_ref[...]
    lfc = _log_sigmoid(gb_ref[s] + fbr_ref[...])
    igr = gat_ref[s] + ibc_ref[...]
    lfr = _log_sigmoid(gbt_ref[s] + fbc_ref[...])
    fc = _split_dot(tri_b, lfc, True)
    ftot = _chunk_totals(rev, fc)
    fr = _split_dot(trit_b, lfr, False)
    gc = ftot - fc + igc
    crow = lambda x, c: x[c * CHUNK:c * CHUNK + 1]
    m_loc = [jnp.max(gc[c * CHUNK:(c + 1) * CHUNK], axis=0, keepdims=True) for c in range(NCB)]
    m = m_ref[s]
    m_enter, a_c, b_c = [None] * NCB, [None] * NCB, [None] * NCB
    for c in _chunk_order(rev):
        m_enter[c] = m
        m_new = jnp.maximum(crow(ftot, c) + m, m_loc[c])
        a_c[c] = jnp.exp(crow(ftot, c) + m - m_new)
        b_c[c] = jnp.exp(m_loc[c] - m_new)
        m = m_new
    m_ref[s] = m
    spread = lambda rows_c: jnp.concatenate(
        [jnp.broadcast_to(r, (CHUNK, LANE)) for r in rows_c], axis=0)
    wc = jnp.exp(gc - spread(m_loc))
    inter_all = fc + spread(m_enter)
    for hh in range(ML_H):
        sl = slice(hh * LANE, (hh + 1) * LANE)
        gl = slice(GATE_OFF + hh, GATE_OFF + hh + 1)
        qh = q[:, sl]
        kh = k[:, sl]
        vaug = jnp.where(ones_col, 1.0, v[:, sl].astype(F32)).astype(BF16)
        qk = lax.dot_general(qh, kh.astype(BF16), _NT, preferred_element_type=F32)
        dlog = fc[:, gl] - fr[hh:hh + 1, :] + igr[hh:hh + 1, :]
        dlog = jnp.where(causal, dlog, -jnp.inf)
        inter = inter_all[:, gl]
        m_t = jnp.maximum(inter, jnp.max(dlog, axis=1, keepdims=True))
        w_inter = jnp.exp(inter - m_t)
        p = (jnp.exp(dlog - m_t) * qk).astype(BF16)
        nd_intra = jnp.dot(p, vaug, preferred_element_type=F32)
        floor = jnp.exp(-m_t)
        kw = (kh * wc[:, gl]).astype(BF16)
        ct = ct_ref[s * ML_H + hh]
        for c in _chunk_order(rev):
            rows = slice(c * CHUNK, (c + 1) * CHUNK)
            nd = nd_intra[rows] + w_inter[rows] * lax.dot_general(qh[rows], ct.astype(BF16), _NT,
                                                                  preferred_element_type=F32)
            den = nd[:, ML_D:ML_D + 1]
            o_ref[s, rows, sl] = nd / jnp.maximum(jnp.abs(den), floor[rows])
            ct = a_c[c][:, gl] * ct + b_c[c][:, gl] * lax.dot_general(vaug[rows], kw[rows], _TN,
                                                                      preferred_element_type=F32)
        ct_ref[s * ML_H + hh] = ct


def _mlstm(rev, z, gates, gates_t, ibr, fbr, ibc, fbc, tri, trit):
    b, lt, _ = z.shape
    nb = lt // TB
    d = int(rev)
    blk = lambda i: _scan_block(d, i, nb)
    ns = _mixer_streams(b)
    zspec = lambda col: pl.BlockSpec((ns, TB, HPAD), lambda bi, i: (bi, blk(i), col))
    gspec = lambda t: pl.BlockSpec((ns, TB, LANE), lambda bi, i: (bi, blk(i), 2 * d + t))
    gtspec = lambda t: pl.BlockSpec((ns, 8, TB), lambda bi, i: (bi, 2 * d + t, blk(i)))
    const = lambda shape: pl.BlockSpec(shape, lambda bi, i: (0,) * len(shape))
    return pl.pallas_call(
        functools.partial(_mlstm_kernel, rev),
        grid=(b // ns, nb),
        in_specs=[zspec(Z_MQ // HPAD), zspec(Z_MK // HPAD), zspec(Z_MV // HPAD),
                  gspec(0), gspec(1), gtspec(0), gtspec(1),
                  const((1, LANE)), const((1, LANE)), const((8, 1)), const((8, 1)),
                  const((TB, TB)), const((TB, TB))],
        out_specs=pl.BlockSpec((ns, TB, HPAD), lambda bi, i: (bi, blk(i), 0)),
        out_shape=jax.ShapeDtypeStruct((b, lt, HPAD), F32),
        scratch_shapes=[pltpu.VMEM((ns * ML_H, LANE, LANE), F32), pltpu.VMEM((ns, 1, LANE), F32)],
        compiler_params=_cparams(("arbitrary", "arbitrary")),
    )(z, z, z, gates, gates, gates_t, gates_t, ibr[d], fbr[d], ibc[d], fbc[d], tri, trit)


def _s5_prep_kernel(arc_ref, aic_ref, arr_ref, air_ref, ldt_ref, br_ref, bi_ref, cr_ref, ci_ref,
                    crt_ref, cit_ref, ar_out, ai_out, k_out, k0_out, vr_out, vi_out, l_out):
    nj = S5_T * S5_HC
    jmap = (lax.broadcasted_iota(jnp.int32, (1, nj), 1) // S5_HC).astype(F32)
    fmap = (lax.broadcasted_iota(jnp.int32, (nj, 1), 0) // S5_HC + 1).astype(F32)
    kmats = []
    for d in range(2):
        dt = jnp.exp(ldt_ref[0, d])
        a_re, a_im = arc_ref[0, d], aic_ref[0, d]
        arc, aic = a_re * dt, a_im * dt
        mag = jnp.exp(arc)
        lre, lim = mag * jnp.cos(aic), mag * jnp.sin(aic)
        den = a_re * a_re + a_im * a_im
        z_re = ((lre - 1.0) * a_re + lim * a_im) / den
        z_im = (lim * a_re - (lre - 1.0) * a_im) / den
        b_re, b_im = br_ref[0, d], bi_ref[0, d]
        bb_re = z_re * b_re - z_im * b_im
        bb_im = z_re * b_im + z_im * b_re
        pmag = jnp.exp(arc * jmap)
        p_re, p_im = pmag * jnp.cos(aic * jmap), pmag * jnp.sin(aic * jmap)
        a_st_re = p_re * bb_re - p_im * bb_im
        a_st_im = p_re * bb_im + p_im * bb_re
        ar_out[0, d] = a_st_re
        ai_out[0, d] = a_st_im
        kmat = (jnp.dot(cr_ref[0, d], a_st_re, preferred_element_type=F32, precision=HIGHEST)
                - jnp.dot(ci_ref[0, d], a_st_im, preferred_element_type=F32, precision=HIGHEST))
        k_out[0, d] = kmat
        kmats.append(kmat)
        arr, air = arr_ref[0, d] * dt, air_ref[0, d] * dt
        fmag = jnp.exp(arr * fmap)
        f_re, f_im = fmag * jnp.cos(air * fmap), fmag * jnp.sin(air * fmap)
        c_re, c_im = crt_ref[0, d], cit_ref[0, d]
        vr_out[0, d] = c_re * f_re - c_im * f_im
        vi_out[0, d] = -(c_re * f_im + c_im * f_re)
        tmag = jnp.exp(arr * float(S5_T))
        t_re, t_im = tmag * jnp.cos(air * float(S5_T)), tmag * jnp.sin(air * float(S5_T))
        l_out[0, d, 0:1, 0:S5_P] = t_re
        l_out[0, d, 0:1, S5_P:2 * S5_P] = t_re
        l_out[0, d, 1:2, 0:S5_P] = -t_im
        l_out[0, d, 1:2, S5_P:2 * S5_P] = t_im
    k0_out[0] = kmats[0][:, 0:S5_HC] + kmats[1][:, 0:S5_HC]


def _s5_operators(a_re, a_im, log_dt, b_re, b_im, c_re, c_im):
    g, p, hc, t = a_re.shape[1], S5_P, S5_HC, S5_T
    nj = t * hc
    gm = lambda x: jnp.moveaxis(x, 0, 1)
    spec = lambda shape: pl.BlockSpec((1,) + shape, lambda i: (i,) + (0,) * len(shape))
    outs = pl.pallas_call(
        _s5_prep_kernel,
        grid=(g,),
        in_specs=[spec((2, p, 1)), spec((2, p, 1)), spec((2, 1, p)), spec((2, 1, p)), spec((2, 1, 1)),
                  spec((2, p, nj)), spec((2, p, nj)), spec((2, hc, p)), spec((2, hc, p)),
                  spec((2, nj, p)), spec((2, nj, p))],
        out_specs=[spec((2, p, nj)), spec((2, p, nj)), spec((2, hc, nj)), spec((hc, hc)),
                   spec((2, nj, p)), spec((2, nj, p)), spec((2, 2, 2 * p))],
        out_shape=[jax.ShapeDtypeStruct((g, 2, p, nj), F32), jax.ShapeDtypeStruct((g, 2, p, nj), F32),
                   jax.ShapeDtypeStruct((g, 2, hc, nj), F32), jax.ShapeDtypeStruct((g, hc, hc), F32),
                   jax.ShapeDtypeStruct((g, 2, nj, p), F32), jax.ShapeDtypeStruct((g, 2, nj, p), F32),
                   jax.ShapeDtypeStruct((g, 2, 2, 2 * p), F32)],
        compiler_params=_cparams(("arbitrary",)),
    )(gm(a_re)[..., None], gm(a_im)[..., None], gm(a_re)[:, :, None, :], gm(a_im)[:, :, None, :],
      gm(log_dt)[..., None, None],
      jnp.tile(gm(b_re), (1, 1, 1, t)), jnp.tile(gm(b_im), (1, 1, 1, t)), gm(c_re), gm(c_im),
      jnp.tile(gm(c_re), (1, 1, t, 1)), jnp.tile(gm(c_im), (1, 1, t, 1)))
    a_st_re, a_st_im, kmat, k0, v_re, v_im, lam_t = outs

    a5 = jnp.stack([a_st_re, a_st_im], axis=2).reshape(g, 2, 2, p, t, hc)
    a5 = jnp.stack([jnp.flip(a5[:, 0], axis=3), a5[:, 1]], axis=1)
    wcat = jnp.transpose(a5, (0, 4, 5, 1, 2, 3)).reshape(g, nj, 4 * p)
    v5 = jnp.stack([v_re, v_im], axis=2).reshape(g, 2, 2, t, hc, p)
    v5 = jnp.stack([v5[:, 0], jnp.flip(v5[:, 1], axis=2)], axis=1)
    vcat = jnp.transpose(v5, (0, 1, 2, 5, 3, 4)).reshape(g, 4 * p, nj)
    k5 = kmat.reshape(g, 2, hc, t, hc)
    ss = np.arange(t)[:, None]
    tt = np.arange(t)[None, :]
    lag = np.abs(tt - ss)
    kf = k5[:, 0][:, :, lag, :]
    kb = k5[:, 1][:, :, lag, :]
    sel_f = jnp.asarray(tt > ss)[None, None, :, :, None]
    sel_b = jnp.asarray(ss > tt)[None, None, :, :, None]
    m5 = jnp.where(sel_f, kf, jnp.where(sel_b, kb, k0[:, :, None, None, :]))
    mfb = jnp.transpose(m5, (0, 2, 4, 3, 1)).reshape(g, nj, nj)
    return wcat.astype(BF16), vcat.astype(BF16), mfb.astype(BF16), lam_t


S5_CB = TB // S5_T
S5_GH = LANE // S5_HC


def _s5_group_rows(ua_ref, ub_ref, ug_ref):
    nbat = ua_ref.shape[0]
    for t in range(S5_T):
        for half, ref in enumerate((ua_ref, ub_ref)):
            rows = jnp.concatenate([ref[bi, pl.ds(t, S5_CB, stride=S5_T), :] for bi in range(nbat)], axis=0)
            for gg in range(S5_GH):
                ug_ref[half * S5_GH + gg, :, t * S5_HC:(t + 1) * S5_HC] = rows[:, gg * S5_HC:(gg + 1) * S5_HC]


def _s5_dx_kernel(ua_ref, ub_ref, w_ref, dxf_ref, dxb_ref, ug_ref):
    nbat = ua_ref.shape[0]
    _s5_group_rows(ua_ref, ub_ref, ug_ref)
    for g in range(S5_G):
        res = jnp.dot(ug_ref[g].astype(BF16), w_ref[g], preferred_element_type=F32)
        for bi in range(nbat):
            rows = slice(bi * S5_CB, (bi + 1) * S5_CB)
            dxf_ref[bi, pl.ds(g, S5_CB, stride=S5_G), :] = res[rows, 0:LANE]
            dxb_ref[bi, pl.ds(g, S5_CB, stride=S5_G), :] = res[rows, LANE:2 * LANE]


def _s5_scan_kernel(dxf_ref, dxb_ref, l_ref, xef_ref, xeb_ref, sf_ref, sb_ref):
    nbat = dxf_ref.shape[0]

    @pl.when(pl.program_id(0) == 0)
    def _():
        sf_ref[...] = jnp.zeros_like(sf_ref)
        sb_ref[...] = jnp.zeros_like(sb_ref)

    tile = lambda r: jnp.concatenate([l_ref[r]] * nbat, axis=0)
    la_f, lb_f, la_b, lb_b = tile(0), tile(1), tile(2), tile(3)

    def step(j, carry):
        xf, xb = carry
        rf = pl.ds(pl.multiple_of(j * S5_G, S5_G), S5_G)
        rb = pl.ds(pl.multiple_of((S5_CB - 1 - j) * S5_G, S5_G), S5_G)
        xef_ref[:, rf, :] = xf.reshape(nbat, S5_G, LANE)
        xeb_ref[:, rb, :] = xb.reshape(nbat, S5_G, LANE)
        dxf = dxf_ref[:, rf, :].reshape(nbat * S5_G, LANE)
        dxb = dxb_ref[:, rb, :].reshape(nbat * S5_G, LANE)
        xf = la_f * xf + lb_f * pltpu.roll(xf, S5_P, 1) + dxf
        xb = la_b * xb + lb_b * pltpu.roll(xb, S5_P, 1) + dxb
        return xf, xb

    xf, xb = lax.fori_loop(0, S5_CB, step, (sf_ref[...], sb_ref[...]))
    sf_ref[...] = xf
    sb_ref[...] = xb


def _s5_y_kernel(ua_ref, ub_ref, xef_ref, xeb_ref, m_ref, v_ref, ya_ref, yb_ref, ug_ref, yg_ref):
    nbat = ua_ref.shape[0]
    _s5_group_rows(ua_ref, ub_ref, ug_ref)
    for g in range(S5_G):
        grows = lambda ref: jnp.concatenate(
            [ref[bi, pl.ds(g, S5_CB, stride=S5_G), :] for bi in range(nbat)], axis=0)
        xe = jnp.concatenate([grows(xef_ref), grows(xeb_ref)], axis=1).astype(BF16)
        yg_ref[g] = (jnp.dot(ug_ref[g].astype(BF16), m_ref[g], preferred_element_type=F32)
                     + jnp.dot(xe, v_ref[g], preferred_element_type=F32))
    for t in range(S5_T):
        for half, ref in enumerate((ya_ref, yb_ref)):
            rows = jnp.concatenate([yg_ref[half * S5_GH + gg, :, t * S5_HC:(t + 1) * S5_HC]
                                    for gg in range(S5_GH)], axis=1)
            for bi in range(nbat):
                ref[bi, pl.ds(t, S5_CB, stride=S5_T), :] = rows[bi * S5_CB:(bi + 1) * S5_CB]


def _s5(ua, ub, wcat, vcat, mfb, lam_t):
    b, lt, _ = ua.shape
    g, nj = S5_G, S5_T * S5_HC
    nb = lt // TB
    srows = S5_CB * g
    tok = pl.BlockSpec((b, TB, LANE), lambda i: (0, i, 0))
    full = lambda a: pl.BlockSpec(a.shape, lambda i: (0,) * a.ndim)
    st_f = pl.BlockSpec((b, srows, LANE), lambda i: (0, i, 0))
    st_b = pl.BlockSpec((b, srows, LANE), lambda i: (0, _scan_block(1, i, nb), 0))
    st_shape = jax.ShapeDtypeStruct((b, (lt // S5_T) * g, LANE), F32)
    dxf, dxb = pl.pallas_call(
        _s5_dx_kernel,
        grid=(nb,),
        in_specs=[tok, tok, full(wcat)],
        out_specs=[st_f, st_f],
        out_shape=[st_shape, st_shape],
        scratch_shapes=[pltpu.VMEM((g, b * S5_CB, nj), F32)],
        compiler_params=_cparams(("arbitrary",)),
    )(ua, ub, wcat)
    lam4 = jnp.moveaxis(lam_t.reshape(g, 4, LANE), 1, 0)
    xef, xeb = pl.pallas_call(
        _s5_scan_kernel,
        grid=(nb,),
        in_specs=[st_f, st_b, full(lam4)],
        out_specs=[st_f, st_b],
        out_shape=[st_shape, st_shape],
        scratch_shapes=[pltpu.VMEM((b * g, LANE), F32)] * 2,
        compiler_params=_cparams(("arbitrary",)),
    )(dxf, dxb, lam4)
    return pl.pallas_call(
        _s5_y_kernel,
        grid=(nb,),
        in_specs=[tok, tok, st_f, st_f, full(mfb), full(vcat)],
        out_specs=[tok, tok],
        out_shape=[jax.ShapeDtypeStruct((b, lt, LANE), F32)] * 2,
        scratch_shapes=[pltpu.VMEM((g, b * S5_CB, nj), F32)] * 2,
        compiler_params=_cparams(("arbitrary",)),
    )(ua, ub, xef, xeb, mfb, vcat)


def _head_ln(o, gain):
    valid = lax.broadcasted_iota(jnp.int32, (1, LANE), 1) < GLA_DV
    outs = []
    for hh in range(GLA_H):
        oh = o[:, hh * LANE:(hh + 1) * LANE]
        mu = jnp.sum(jnp.where(valid, oh, 0.0), axis=-1, keepdims=True) * (1.0 / GLA_DV)
        oc = jnp.where(valid, oh - mu, 0.0)
        var = jnp.sum(oc * oc, axis=-1, keepdims=True) * (1.0 / GLA_DV)
        outs.append(oc * lax.rsqrt(var + LN_EPS))
    return jnp.concatenate(outs, axis=-1) * gain


OUT_ROWS = 2


def _out_kernel(alpha, hc_ref, hx_ref, mod_ref, ya_ref, yb_ref, ua_ref, ub_ref, gr_ref, mo_ref, gf_ref, gb_ref,
                mf_ref, mb_ref, dsk_ref, wglu_ref, bglu_ref, gg_ref, mg_ref, wout_ref, l1g_ref, l1b_ref,
                wup_ref, h1_ref, a_ref, v_ref, mix_ref):
    nrow = hx_ref.shape[0]
    for r in range(nrow):
        ys = jnp.concatenate([ya_ref[r], yb_ref[r]], axis=1)
        u = jnp.concatenate([ua_ref[r], ub_ref[r]], axis=1)
        y1 = jax.nn.gelu(ys + dsk_ref[...] * u)
        glu = jnp.dot(y1.astype(BF16), wglu_ref[...], preferred_element_type=F32) + bglu_ref[...]
        mix_ref[r, :, 0:S5_W] = (y1 * jax.nn.sigmoid(glu)).astype(BF16)
        gate = gr_ref[r].astype(F32)
        glo = _head_ln(gf_ref[r] + gb_ref[r], gg_ref[...])
        mix_ref[r, :, S5_W:S5_W + HPAD] = (gate * jax.nn.sigmoid(gate) * glo).astype(BF16)
        mlo = _head_ln(mf_ref[r] + mb_ref[r], mg_ref[...])
        mix_ref[r, :, S5_W + HPAD:MIX_W] = (jax.nn.sigmoid(mo_ref[r].astype(F32)) * mlo).astype(BF16)
    for r in range(nrow):
        mod = mod_ref[r, 0]
        mixed = jnp.dot(mix_ref[r], wout_ref[...], preferred_element_type=F32)
        y = alpha * _stream_block(hc_ref, hx_ref, r) + mod[2:3] * mixed
        h1 = _ln_rows(y) * l1g_ref[...] + l1b_ref[...]
        h1_ref[r] = h1
        hm = (_ln_rows(h1) * (1.0 + mod[4:5]) + mod[3:4]).astype(BF16)
        a_ref[r] = jnp.dot(hm, wup_ref[:, 0:D_FF], preferred_element_type=F32).astype(BF16)
        v_ref[r] = jnp.dot(hm, wup_ref[:, D_FF:2 * D_FF], preferred_element_type=F32).astype(BF16)


def _out_proj(alpha, h, modtab, ys5, us5, z, ogla, oml, dsk, wglu, bglu, gg, mg, wout, l1g, l1b, wup):
    d = D_MODEL
    rows = OUT_ROWS if modtab.shape[0] % OUT_ROWS == 0 else 1
    h_ops, h_specs, (b, lt) = _stream_operands(h, rows)
    nb = lt // TB
    const = lambda shape: pl.BlockSpec(shape, lambda bi, j: (0,) * len(shape), pipeline_mode=pl.Buffered(1))
    tok = lambda w, col=0: pl.BlockSpec((rows, TB, w), lambda bi, j: (bi, j, col))
    return pl.pallas_call(
        functools.partial(_out_kernel, alpha),
        grid=(b // rows, nb),
        in_specs=h_specs + [
                  pl.BlockSpec((rows, 1, 6, d), lambda bi, j: (bi, jnp.minimum(j, 1), 0, 0)),
                  tok(LANE), tok(LANE), tok(LANE), tok(LANE), tok(HPAD, Z_GR // HPAD), tok(HPAD, Z_MO // HPAD),
                  tok(HPAD), tok(HPAD), tok(HPAD), tok(HPAD),
                  const((1, S5_W)), const((S5_W, S5_W)), const((1, S5_W)), const((1, HPAD)), const((1, HPAD)),
                  const((MIX_W, d)), const((1, d)), const((1, d)), const((d, 2 * D_FF))],
        out_specs=[tok(d), tok(D_FF), tok(D_FF)],
        out_shape=[jax.ShapeDtypeStruct((b, lt, d), F32), jax.ShapeDtypeStruct((b, lt, D_FF), BF16),
                   jax.ShapeDtypeStruct((b, lt, D_FF), BF16)],
        scratch_shapes=[pltpu.VMEM((rows, TB, MIX_W), BF16)],
        compiler_params=_cparams(("arbitrary", "arbitrary")),
    )(*h_ops, modtab, ys5[0], ys5[1], us5[0], us5[1], z, z, ogla[0], ogla[1], oml[0], oml[1], dsk, wglu, bglu,
      gg, mg, wout, l1g, l1b, wup)


FF_CW = 256
GELU_E0 = -2.0 * math.sqrt(2.0 / math.pi) * math.log2(math.e)
GELU_E1 = 0.044715 * GELU_E0


def _ffn_kernel(alpha, nb, h_ref, mod_ref, a_ref, ap_ref, an_ref, v_ref, wc_ref, bc_ref, wd_ref,
                l2g_ref, l2b_ref, o_ref, g_ref):
    j = pl.program_id(1)
    is_lat = j > 0
    ngrp = TB // GRID_W
    grp = lax.broadcasted_iota(jnp.int32, (ngrp, 8, 1), 0)
    sub = lax.broadcasted_iota(jnp.int32, (ngrp, 8, 1), 1)
    edge_l = jnp.where((sub == 0) & (is_lat | (grp == 0)), 0.0, 1.0)
    edge_r = jnp.where((sub == 7) & (is_lat | (grp == ngrp - 1)), 0.0, 1.0)

    def mask_group_tile(x, first, edge):
        x = x.reshape(ngrp, GRID_W, x.shape[1])
        if first:
            x = jnp.concatenate([x[:, :8] * edge, x[:, 8:]], axis=1)
        else:
            x = jnp.concatenate([x[:, :GRID_W - 8], x[:, GRID_W - 8:] * edge], axis=1)
        return x.reshape(TB, x.shape[2])

    vert = is_lat.astype(F32)
    up_ok = (j > 1).astype(BF16)
    dn_ok = (j < nb - 1).astype(BF16)
    for r in range(h_ref.shape[0]):
        for c0 in range(0, D_FF, FF_CW):
            cs = slice(c0, c0 + FF_CW)
            a = a_ref[r, :, cs]
            up = jnp.concatenate([ap_ref[r, :, cs] * up_ok, a[:TB - GRID_W]], axis=0)
            dn = jnp.concatenate([a[GRID_W:], an_ref[r, :, cs] * dn_ok], axis=0)
            w = wc_ref[:, cs]
            wb = w.astype(BF16)
            wvb = (w * vert).astype(BF16)
            left = (up * wvb[0:1] + a * wb[3:4] + dn * wvb[6:7]).astype(F32)
            mid = (up * wvb[1:2] + a * wb[4:5] + dn * wvb[7:8]).astype(F32)
            right = (up * wvb[2:3] + a * wb[5:6] + dn * wvb[8:9]).astype(F32)
            acc = ((mid + bc_ref[:, cs]) + mask_group_tile(pltpu.roll(left, 1, 0), True, edge_l)
                   + mask_group_tile(pltpu.roll(right, TB - 1, 0), False, edge_r))
            e = jnp.exp2(acc * (GELU_E0 + GELU_E1 * (acc * acc)))
            g_ref[r, :, cs] = ((acc * v_ref[r, :, cs].astype(F32)) / (1.0 + e)).astype(BF16)
    for r in range(h_ref.shape[0]):
        f = jnp.dot(g_ref[r], wd_ref[...], preferred_element_type=F32)
        y = alpha * h_ref[r] + mod_ref[r, 0][5:6] * f
        o_ref[r] = _ln_rows(y) * l2g_ref[...] + l2b_ref[...]


def _ffn(alpha, h1, modtab, a, v, wc, bc, wd, l2g, l2b, latent_only):
    b, lt, d = h1.shape
    nb = lt // TB
    rpb = TB // GRID_W
    nrow = lt // GRID_W
    rows = OUT_ROWS if b % OUT_ROWS == 0 else 1
    const = lambda shape: pl.BlockSpec(shape, lambda bi, j: (0,) * len(shape), pipeline_mode=pl.Buffered(1))
    tok = lambda w: pl.BlockSpec((rows, TB, w), lambda bi, j: (bi, j, 0))
    if latent_only:
        out_spec = pl.BlockSpec((rows, TB, d), lambda bi, j: (bi, jnp.maximum(j - 1, 0), 0))
        out_len = lt - TB
    else:
        out_spec, out_len = tok(d), lt
    return pl.pallas_call(
        functools.partial(_ffn_kernel, alpha, nb),
        grid=(b // rows, nb),
        in_specs=[tok(d), pl.BlockSpec((rows, 1, 6, d), lambda bi, j: (bi, jnp.minimum(j, 1), 0, 0)),
                  tok(D_FF),
                  pl.BlockSpec((rows, GRID_W, D_FF), lambda bi, j: (bi, jnp.maximum(j * rpb - 1, 0), 0)),
                  pl.BlockSpec((rows, GRID_W, D_FF),
                               lambda bi, j: (bi, jnp.minimum((j + 1) * rpb, nrow - 1), 0)),
                  tok(D_FF), const((9, D_FF)), const((1, D_FF)), const((D_FF, d)),
                  const((1, d)), const((1, d))],
        out_specs=out_spec,
        out_shape=jax.ShapeDtypeStruct((b, out_len, d), F32),
        scratch_shapes=[pltpu.VMEM((rows, TB, D_FF), BF16)],
        compiler_params=_cparams(("arbitrary", "arbitrary")),
    )(h1, modtab, a, a, a, v, wc, bc, wd, l2g, l2b)


def _pad_heads(w, nh, dh):
    lead = w.shape[:-1]
    w = w.reshape(lead + (nh, dh))
    w = jnp.pad(w, [(0, 0)] * len(lead) + [(0, 0), (0, LANE - dh)])
    return w.reshape(lead + (nh * LANE,))


def _pad_head_rows(w, nh, dh):
    return jnp.moveaxis(_pad_heads(jnp.moveaxis(w, 0, -1), nh, dh), -1, 0)


def _layer_weights(w_in, gla_w_a2, gla_b_a, ml_i_bias, ml_f_bias, w_out):
    d = w_in.shape[0]
    offs = np.cumsum((0, S5_W, GLA_H * GLA_DK, GLA_H * GLA_DK, GLA_W, GLA_W, 2 * GLA_RANK,
                      ML_W, ML_W, ML_W, ML_W, 2 * ML_H, 2 * ML_H))
    part = lambda i: w_in[:, offs[i]:offs[i + 1]]
    w_p = jnp.concatenate([
        _pad_heads(part(1), GLA_H, GLA_DK), _pad_heads(part(2), GLA_H, GLA_DK),
        _pad_heads(part(3), GLA_H, GLA_DV), _pad_heads(part(4), GLA_H, GLA_DV),
        _pad_heads(part(6), ML_H, ML_D), _pad_heads(part(7), ML_H, ML_D),
        _pad_heads(part(8), ML_H, ML_D), _pad_heads(part(9), ML_H, ML_D)], axis=1).astype(BF16)
    w_u = part(0).astype(BF16)
    lr, ig, fg = part(5), part(10), part(11)
    zeros = lambda n: jnp.zeros((d, n), w_in.dtype)
    tiles = []
    for dd in range(2):
        tiles += [lr[:, dd * GLA_RANK:(dd + 1) * GLA_RANK], ig[:, dd * ML_H:(dd + 1) * ML_H],
                  zeros(LANE - GLA_RANK - ML_H),
                  zeros(GATE_OFF), fg[:, dd * ML_H:(dd + 1) * ML_H], zeros(LANE - GATE_OFF - ML_H)]
    w_g = jnp.concatenate(tiles, axis=1).astype(BF16)
    rows_t = []
    for dd in range(2):
        for gate in (ig, fg):
            rows_t += [gate[:, dd * ML_H:(dd + 1) * ML_H], zeros(8 - ML_H)]
    w_gt = jnp.concatenate(rows_t, axis=1).T.astype(BF16)
    wa = jnp.pad(_pad_heads(gla_w_a2, GLA_H, GLA_DK), ((0, 0), (0, LANE - GLA_RANK), (0, 0)))
    ba = _pad_heads(gla_b_a, GLA_H, GLA_DK)[:, None, :]
    gate_row = lambda bias: jnp.pad(bias, ((0, 0), (GATE_OFF, LANE - GATE_OFF - ML_H)))[:, None, :]
    gate_col = lambda bias: jnp.pad(bias, ((0, 0), (0, 8 - ML_H)))[:, :, None]
    wout_p = jnp.concatenate([
        w_out[:S5_W], _pad_head_rows(w_out[S5_W:S5_W + GLA_W], GLA_H, GLA_DV),
        _pad_head_rows(w_out[S5_W + GLA_W:], ML_H, ML_D)], axis=0).astype(BF16)
    return dict(w_p=w_p, w_u=w_u, w_g=w_g, w_gt=w_gt, wa=wa, ba=ba,
                ibr=gate_row(ml_i_bias), fbr=gate_row(ml_f_bias),
                ibc=gate_col(ml_i_bias), fbc=gate_col(ml_f_bias), wout=wout_p)


def kernel(x, c, ctx, c_ctx, w_ada, b_ada, w_in, s5_a_re, s5_a_im, s5_log_dt, s5_b_re, s5_b_im, s5_c_re, s5_c_im, s5_d, s5_w_glu, s5_b_glu, gla_w_a2, gla_b_a, gla_g, ml_i_bias, ml_f_bias, ml_g, w_out, ln1_g, ln1_b, w_up, w_dconv, b_dconv, w_down, ln2_g, ln2_b):
    b, seq, d = x.shape
    ctx_len = ctx.shape[1]
    depth = w_in.shape[0]
    assert d == D_MODEL and ctx_len == TB and seq % TB == 0 and b <= 7
    alpha = (2.0 * depth) ** 0.25

    cc = jnp.zeros((8, d), F32).at[:b].set(c).at[b].set(c_ctx)
    mod = _modulation(cc, w_ada, b_ada).reshape(depth, 8, 6, d)
    modtab = jnp.stack([jnp.broadcast_to(mod[:, b:b + 1], (depth, b, 6, d)), mod[:, :b]], axis=2)

    eye = np.eye(NCB, dtype=np.float32)
    low = np.kron(eye, np.tril(np.ones((CHUNK, CHUNK), np.float32)))
    tri = (jnp.asarray(low), jnp.asarray(low.T))

    fold = lambda t: jnp.moveaxis(t, 0, 1).reshape((2, depth * S5_G) + t.shape[3:])
    s5_ops = _s5_operators(*(fold(t) for t in (s5_a_re, s5_a_im, s5_log_dt, s5_b_re, s5_b_im,
                                                s5_c_re, s5_c_im)))

    h = (ctx, x)
    for l in range(depth):
        lw = _layer_weights(w_in[l], gla_w_a2[l], gla_b_a[l], ml_i_bias[l], ml_f_bias[l], w_out[l])
        z, ua, ub, gates, gates_t = _in_proj(h, modtab[l], lw["w_p"], lw["w_u"], lw["w_g"], lw["w_gt"])
        ogla = [_gla(rev, z, gates, lw["wa"], lw["ba"], tri[rev]) for rev in (0, 1)]
        oml = [_mlstm(rev, z, gates, gates_t, lw["ibr"], lw["fbr"], lw["ibc"], lw["fbc"],
                      tri[rev], tri[1 - rev]) for rev in (0, 1)]
        wcat, vcat, mfb, lam_t = (t[l * S5_G:(l + 1) * S5_G] for t in s5_ops)
        ys5 = _s5(ua, ub, wcat, vcat, mfb, lam_t)
        h1, a, v = _out_proj(alpha, h, modtab[l], ys5, (ua, ub), z, ogla, oml, s5_d[l][None],
                             s5_w_glu[l].astype(BF16),
                             s5_b_glu[l][None], _pad_heads(gla_g[l], GLA_H, GLA_DV)[None],
                             _pad_heads(ml_g[l], ML_H, ML_D)[None], lw["wout"], ln1_g[l][None],
                             ln1_b[l][None], w_up[l].astype(BF16))
        h = _ffn(alpha, h1, modtab[l], a, v, w_dconv[l].reshape(9, D_FF), b_dconv[l][None],
                 w_down[l].astype(BF16), ln2_g[l][None], ln2_b[l][None], latent_only=(l == depth - 1))
    return h
```

```python
import functools
import math

import numpy as np
import jax
import jax.numpy as jnp
from jax import lax
from jax.experimental import pallas as pl
from jax.experimental.pallas import tpu as pltpu

F32 = jnp.float32
BF16 = jnp.bfloat16
HIGHEST = lax.Precision.HIGHEST

D_MODEL = 1024
GRID_W = 64
CHUNK = 64
S5_W = D_MODEL // 4
S5_HC = 16
S5_G = S5_W // S5_HC
S5_P = 64
S5_T = 16
GLA_W = 3 * D_MODEL // 8
GLA_H = 4
GLA_DV = GLA_W // GLA_H
GLA_DK = GLA_DV // 2
GLA_RANK = 16
GLA_TAU = 16.0
ML_W = 3 * D_MODEL // 8
ML_H = 4
ML_D = ML_W // ML_H
D_FF = ((8 * D_MODEL // 3 + 127) // 128) * 128
LN_EPS = 1e-5

LANE = 128
HPAD = GLA_H * LANE
TB = 256
NCB = TB // CHUNK
Z_GQ, Z_GK, Z_GV, Z_GR = 0, HPAD, 2 * HPAD, 3 * HPAD
Z_MQ, Z_MK, Z_MV, Z_MO = 4 * HPAD, 5 * HPAD, 6 * HPAD, 7 * HPAD
NZM = 8 * HPAD
NGATE = 4 * LANE
NGATE_T = 32
GATE_OFF = 16
MIX_W = S5_W + 2 * HPAD

VMEM_LIMIT = 56 * 1024 * 1024


def _cparams(sem):
    return pltpu.CompilerParams(dimension_semantics=sem, vmem_limit_bytes=VMEM_LIMIT)


def _ln_rows(x):
    mu = jnp.mean(x, axis=-1, keepdims=True)
    xc = x - mu
    var = jnp.mean(xc * xc, axis=-1, keepdims=True)
    return xc * lax.rsqrt(var + LN_EPS)


def _log_sigmoid(x):
    return jnp.minimum(x, 0.0) - jnp.log(1.0 + jnp.exp(-jnp.abs(x)))


def _split_dot(tri_b, x, left):
    hi = x.astype(BF16)
    lo = (x - hi.astype(F32)).astype(BF16)
    if left:
        return (jnp.dot(tri_b, hi, preferred_element_type=F32)
                + jnp.dot(tri_b, lo, preferred_element_type=F32))
    return (jnp.dot(hi, tri_b, preferred_element_type=F32)
            + jnp.dot(lo, tri_b, preferred_element_type=F32))


_NT = (((1,), (1,)), ((), ()))
_TN = (((0,), (0,)), ((), ()))


def _mod_kernel(c_ref, w_ref, b_ref, o_ref):
    cc = c_ref[...]
    s = cc * jax.nn.sigmoid(cc)
    o_ref[0] = jnp.dot(s, w_ref[0], preferred_element_type=F32, precision=HIGHEST) + b_ref[0]


def _modulation(cc, w_ada, b_ada):
    depth, d, n = w_ada.shape
    nb = 4
    bn = n // nb
    return pl.pallas_call(
        _mod_kernel,
        grid=(depth, nb),
        in_specs=[pl.BlockSpec((8, d), lambda l, j: (0, 0)),
                  pl.BlockSpec((1, d, bn), lambda l, j: (l, 0, j)),
                  pl.BlockSpec((1, 1, bn), lambda l, j: (l, 0, j))],
        out_specs=pl.BlockSpec((1, 8, bn), lambda l, j: (l, 0, j)),
        out_shape=jax.ShapeDtypeStruct((depth, 8, n), F32),
        compiler_params=_cparams(("arbitrary", "arbitrary")),
    )(cc, w_ada, b_ada.reshape(depth, 1, n))


def _stream_operands(h, rows=1):
    if isinstance(h, tuple):
        ctx, x = h
        total = ctx.shape[1] + x.shape[1]
        run = pl.BlockSpec((rows, TB, x.shape[2]), lambda bi, j: (bi, jnp.maximum(j - 1, 0), 0))
    else:
        ctx = x = h
        total = x.shape[1]
        run = pl.BlockSpec((rows, TB, x.shape[2]), lambda bi, j: (bi, j, 0))
    head = pl.BlockSpec((rows, TB, x.shape[2]), lambda bi, j: (bi, 0, 0))
    return (ctx, x), [head, run], (x.shape[0], total)


def _stream_block(hc_ref, hx_ref, r=0):
    return jnp.where(pl.program_id(1) == 0, hc_ref[r], hx_ref[r])


IN_ROWS = 2


def _in_kernel(hc_ref, hx_ref, mod_ref, w_ref, wu_ref, wg_ref, wgt_ref, z_ref, ua_ref, ub_ref, g_ref, gt_ref):
    hms = []
    for r in range(hx_ref.shape[0]):
        mod = mod_ref[r, 0]
        hms.append((_ln_rows(_stream_block(hc_ref, hx_ref, r)) * (1.0 + mod[1:2]) + mod[0:1]).astype(BF16))
    for r, hm in enumerate(hms):
        for c0 in range(0, NZM, HPAD):
            z_ref[r, :, c0:c0 + HPAD] = jnp.dot(hm, w_ref[:, c0:c0 + HPAD],
                                                preferred_element_type=F32).astype(BF16)
        u = jnp.dot(hm, wu_ref[...], preferred_element_type=F32)
        ua_ref[r] = u[:, :LANE]
        ub_ref[r] = u[:, LANE:]
        g_ref[r] = jnp.dot(hm, wg_ref[...], preferred_element_type=F32)
        gt_ref[r] = lax.dot_general(wgt_ref[...], hm, _NT, preferred_element_type=F32)


def _in_proj(h, modtab, w_p, w_u, w_g, w_gt):
    d = D_MODEL
    rows = IN_ROWS if modtab.shape[0] % IN_ROWS == 0 else 1
    h_ops, h_specs, (b, lt) = _stream_operands(h, rows)
    nb = lt // TB
    const = lambda shape: pl.BlockSpec(shape, lambda bi, j: (0,) * len(shape))
    tok = lambda w: pl.BlockSpec((rows, TB, w), lambda bi, j: (bi, j, 0))
    return pl.pallas_call(
        _in_kernel,
        grid=(b // rows, nb),
        in_specs=h_specs + [
                  pl.BlockSpec((rows, 1, 6, d), lambda bi, j: (bi, jnp.minimum(j, 1), 0, 0)),
                  const((d, NZM)), const((d, S5_W)), const((d, NGATE)), const((NGATE_T, d))],
        out_specs=[tok(NZM), tok(LANE), tok(LANE), tok(NGATE),
                   pl.BlockSpec((rows, NGATE_T, TB), lambda bi, j: (bi, 0, j))],
        out_shape=[jax.ShapeDtypeStruct((b, lt, NZM), BF16),
                   jax.ShapeDtypeStruct((b, lt, LANE), F32),
                   jax.ShapeDtypeStruct((b, lt, LANE), F32),
                   jax.ShapeDtypeStruct((b, lt, NGATE), F32),
                   jax.ShapeDtypeStruct((b, NGATE_T, lt), F32)],
        compiler_params=_cparams(("arbitrary", "arbitrary")),
    )(*h_ops, modtab, w_p, w_u, w_g, w_gt)


def _scan_block(d, i, nb):
    return jnp.where(d == 0, i, jnp.where(i == 0, 0, nb - i))


MIXER_STREAMS = 4


def _mixer_streams(b):
    return MIXER_STREAMS if b % MIXER_STREAMS == 0 else 1


def _chunk_order(rev):
    return range(NCB - 1, -1, -1) if rev else range(NCB)


def _chunk_totals(rev, cum):
    last = 0 if rev else CHUNK - 1
    return jnp.concatenate(
        [jnp.broadcast_to(cum[c * CHUNK + last:c * CHUNK + last + 1], (CHUNK, cum.shape[1]))
         for c in range(NCB)], axis=0)


def _gla_kernel(rev, q_ref, k_ref, v_ref, g_ref, wa_ref, ba_ref, tri_ref, o_ref, st_ref):
    @pl.when(pl.program_id(1) == 0)
    def _():
        st_ref[...] = jnp.zeros_like(st_ref)

    tri = tri_ref[...]
    tri_b = tri.astype(BF16)
    causal = tri[:CHUNK, :CHUNK] > 0.0
    wa_b = wa_ref[...].astype(BF16)
    for s in range(q_ref.shape[0]):
        q = q_ref[s].astype(F32) * (GLA_DK ** -0.5)
        k = k_ref[s].astype(F32)
        v = v_ref[s]
        zz = jnp.dot(g_ref[s].astype(BF16), wa_b, preferred_element_type=F32) + ba_ref[...]
        loga = _log_sigmoid(zz) * (1.0 / GLA_TAU)
        bcum = jnp.dot(tri_b, loga.astype(BF16), preferred_element_type=F32)
        btot = _chunk_totals(rev, bcum)
        qd = (q * jnp.exp(bcum)).astype(BF16)
        kd = (k * jnp.exp(-bcum)).astype(BF16)
        kdec = (k * jnp.exp(btot - bcum)).astype(BF16)
        dec = jnp.exp(btot)
        for hh in range(GLA_H):
            sl = slice(hh * LANE, (hh + 1) * LANE)
            qh, kh, vh, kdh = qd[:, sl], kd[:, sl], v[:, sl], kdec[:, sl]
            st = st_ref[s * GLA_H + hh]
            for c in _chunk_order(rev):
                rows = slice(c * CHUNK, (c + 1) * CHUNK)
                att = lax.dot_general(qh[rows], kh[rows], _NT, preferred_element_type=F32)
                att = jnp.where(causal, att, 0.0).astype(BF16)
                o_ref[s, rows, sl] = (jnp.dot(att, vh[rows], preferred_element_type=F32)
                                      + lax.dot_general(qh[rows], st.astype(BF16), _NT,
                                                        preferred_element_type=F32))
                st = st * dec[c * CHUNK:c * CHUNK + 1, sl] + lax.dot_general(
                    vh[rows], kdh[rows], _TN, preferred_element_type=F32)
            st_ref[s * GLA_H + hh] = st


def _gla(rev, z, gates, wa, ba, tri):
    b, lt, _ = z.shape
    nb = lt // TB
    d = int(rev)
    blk = lambda i: _scan_block(d, i, nb)
    ns = _mixer_streams(b)
    zspec = lambda col: pl.BlockSpec((ns, TB, HPAD), lambda bi, i: (bi, blk(i), col))
    const = lambda shape: pl.BlockSpec(shape, lambda bi, i: (0,) * len(shape))
    return pl.pallas_call(
        functools.partial(_gla_kernel, rev),
        grid=(b // ns, nb),
        in_specs=[zspec(Z_GQ // HPAD), zspec(Z_GK // HPAD), zspec(Z_GV // HPAD),
                  pl.BlockSpec((ns, TB, LANE), lambda bi, i: (bi, blk(i), 2 * d)),
                  const((LANE, HPAD)), const((1, HPAD)), const((TB, TB))],
        out_specs=pl.BlockSpec((ns, TB, HPAD), lambda bi, i: (bi, blk(i), 0)),
        out_shape=jax.ShapeDtypeStruct((b, lt, HPAD), F32),
        scratch_shapes=[pltpu.VMEM((ns * GLA_H, LANE, LANE), F32)],
        compiler_params=_cparams(("arbitrary", "arbitrary")),
    )(z, z, z, gates, wa[d], ba[d], tri)


def _mlstm_kernel(rev, q_ref, k_ref, v_ref, ga_ref, gb_ref, gat_ref, gbt_ref, ibr_ref, fbr_ref,
                  ibc_ref, fbc_ref, tri_ref, trit_ref, o_ref, ct_ref, m_ref):
    @pl.when(pl.program_id(1) == 0)
    def _():
        ct_ref[...] = jnp.zeros_like(ct_ref)
        m_ref[...] = jnp.zeros_like(m_ref)

    tri = tri_ref[...]
    tri_b = tri.astype(BF16)
    trit_b = trit_ref[...].astype(BF16)
    causal = tri > 0.0
    ones_col = lax.broadcasted_iota(jnp.int32, (TB, LANE), 1) == ML_D
    for s in range(q_ref.shape[0]):
        _mlstm_stream(rev, s, tri_b, trit_b, causal, ones_col, q_ref, k_ref, v_ref, ga_ref, gb_ref,
                      gat_ref, gbt_ref, ibr_ref, fbr_ref, ibc_ref, fbc_ref, o_ref, ct_ref, m_ref)


def _mlstm_stream(rev, s, tri_b, trit_b, causal, ones_col, q_ref, k_ref, v_ref, ga_ref, gb_ref,
                  gat_ref, gbt_ref, ibr_ref, fbr_ref, ibc_ref, fbc_ref, o_ref, ct_ref, m_ref):
    q = q_ref[s]
    k = k_ref[s].astype(F32) * (ML_D ** -0.5)
    v = v_ref[s]
    igc = ga_ref[s] + ibr_ref[...]
    lfc = _log_sigmoid(gb_ref[s] + fbr_ref[...])
    igr = gat_ref[s] + ibc_ref[...]
    lfr = _log_sigmoid(gbt_ref[s] + fbc_ref[...])
    fc = _split_dot(tri_b, lfc, True)
    ftot = _chunk_totals(rev, fc)
    fr = _split_dot(trit_b, lfr, False)
    gc = ftot - fc + igc
    crow = lambda x, c: x[c * CHUNK:c * CHUNK + 1]
    m_loc = [jnp.max(gc[c * CHUNK:(c + 1) * CHUNK], axis=0, keepdims=True) for c in range(NCB)]
    m = m_ref[s]
    m_enter, a_c, b_c = [None] * NCB, [None] * NCB, [None] * NCB
    for c in _chunk_order(rev):
        m_enter[c] = m
        m_new = jnp.maximum(crow(ftot, c) + m, m_loc[c])
        a_c[c] = jnp.exp(crow(ftot, c) + m - m_new)
        b_c[c] = jnp.exp(m_loc[c] - m_new)
        m = m_new
    m_ref[s] = m
    spread = lambda rows_c: jnp.concatenate(
        [jnp.broadcast_to(r, (CHUNK, LANE)) for r in rows_c], axis=0)
    wc = jnp.exp(gc - spread(m_loc))
    key_w = igr - fr
    me_all = spread(m_enter)
    for hh in range(ML_H):
        sl = slice(hh * LANE, (hh + 1) * LANE)
        gl = slice(GATE_OFF + hh, GATE_OFF + hh + 1)
        qh = q[:, sl]
        kh = k[:, sl]
        khb = kh.astype(BF16)
        vaug = jnp.where(ones_col, 1.0, v[:, sl].astype(F32)).astype(BF16)
        qk = lax.dot_general(qh, khb, _NT, preferred_element_type=F32)
        w_log = jnp.where(causal, key_w[hh:hh + 1, :], -jnp.inf)
        me = me_all[:, gl]
        mu = jnp.maximum(me, jnp.max(w_log, axis=1, keepdims=True))
        p = (jnp.exp(w_log - mu) * qk).astype(BF16)
        nd_intra = jnp.dot(p, vaug, preferred_element_type=F32)
        w_inter = jnp.exp(me - mu)
        floor = jnp.exp(-(fc[:, gl] + mu))
        kw = (kh * wc[:, gl]).astype(BF16)
        ct = ct_ref[s * ML_H + hh]
        for c in _chunk_order(rev):
            rows = slice(c * CHUNK, (c + 1) * CHUNK)
            nd = nd_intra[rows] + w_inter[rows] * lax.dot_general(qh[rows], ct.astype(BF16), _NT,
                                                                  preferred_element_type=F32)
            den = nd[:, ML_D:ML_D + 1]
            o_ref[s, rows, sl] = nd / jnp.maximum(jnp.abs(den), floor[rows])
            ct = a_c[c][:, gl] * ct + b_c[c][:, gl] * lax.dot_general(vaug[rows], kw[rows], _TN,
                                                                      preferred_element_type=F32)
        ct_ref[s * ML_H + hh] = ct


def _mlstm(rev, z, gates, gates_t, ibr, fbr, ibc, fbc, tri, trit):
    b, lt, _ = z.shape
    nb = lt // TB
    d = int(rev)
    blk = lambda i: _scan_block(d, i, nb)
    ns = _mixer_streams(b)
    zspec = lambda col: pl.BlockSpec((ns, TB, HPAD), lambda bi, i: (bi, blk(i), col))
    gspec = lambda t: pl.BlockSpec((ns, TB, LANE), lambda bi, i: (bi, blk(i), 2 * d + t))
    gtspec = lambda t: pl.BlockSpec((ns, 8, TB), lambda bi, i: (bi, 2 * d + t, blk(i)))
    const = lambda shape: pl.BlockSpec(shape, lambda bi, i: (0,) * len(shape))
    return pl.pallas_call(
        functools.partial(_mlstm_kernel, rev),
        grid=(b // ns, nb),
        in_specs=[zspec(Z_MQ // HPAD), zspec(Z_MK // HPAD), zspec(Z_MV // HPAD),
                  gspec(0), gspec(1), gtspec(0), gtspec(1),
                  const((1, LANE)), const((1, LANE)), const((8, 1)), const((8, 1)),
                  const((TB, TB)), const((TB, TB))],
        out_specs=pl.BlockSpec((ns, TB, HPAD), lambda bi, i: (bi, blk(i), 0)),
        out_shape=jax.ShapeDtypeStruct((b, lt, HPAD), F32),
        scratch_shapes=[pltpu.VMEM((ns * ML_H, LANE, LANE), F32), pltpu.VMEM((ns, 1, LANE), F32)],
        compiler_params=_cparams(("arbitrary", "arbitrary")),
    )(z, z, z, gates, gates, gates_t, gates_t, ibr[d], fbr[d], ibc[d], fbc[d], tri, trit)


def _s5_prep_kernel(arc_ref, aic_ref, arr_ref, air_ref, ldt_ref, br_ref, bi_ref, cr_ref, ci_ref,
                    crt_ref, cit_ref, ar_out, ai_out, k_out, k0_out, vr_out, vi_out, l_out):
    nj = S5_T * S5_HC
    jmap = (lax.broadcasted_iota(jnp.int32, (1, nj), 1) // S5_HC).astype(F32)
    fmap = (lax.broadcasted_iota(jnp.int32, (nj, 1), 0) // S5_HC + 1).astype(F32)
    kmats = []
    for d in range(2):
        dt = jnp.exp(ldt_ref[0, d])
        a_re, a_im = arc_ref[0, d], aic_ref[0, d]
        arc, aic = a_re * dt, a_im * dt
        mag = jnp.exp(arc)
        lre, lim = mag * jnp.cos(aic), mag * jnp.sin(aic)
        den = a_re * a_re + a_im * a_im
        z_re = ((lre - 1.0) * a_re + lim * a_im) / den
        z_im = (lim * a_re - (lre - 1.0) * a_im) / den
        b_re, b_im = br_ref[0, d], bi_ref[0, d]
        bb_re = z_re * b_re - z_im * b_im
        bb_im = z_re * b_im + z_im * b_re
        pmag = jnp.exp(arc * jmap)
        p_re, p_im = pmag * jnp.cos(aic * jmap), pmag * jnp.sin(aic * jmap)
        a_st_re = p_re * bb_re - p_im * bb_im
        a_st_im = p_re * bb_im + p_im * bb_re
        ar_out[0, d] = a_st_re
        ai_out[0, d] = a_st_im
        kmat = (jnp.dot(cr_ref[0, d], a_st_re, preferred_element_type=F32, precision=HIGHEST)
                - jnp.dot(ci_ref[0, d], a_st_im, preferred_element_type=F32, precision=HIGHEST))
        k_out[0, d] = kmat
        kmats.append(kmat)
        arr, air = arr_ref[0, d] * dt, air_ref[0, d] * dt
        fmag = jnp.exp(arr * fmap)
        f_re, f_im = fmag * jnp.cos(air * fmap), fmag * jnp.sin(air * fmap)
        c_re, c_im = crt_ref[0, d], cit_ref[0, d]
        vr_out[0, d] = c_re * f_re - c_im * f_im
        vi_out[0, d] = -(c_re * f_im + c_im * f_re)
        tmag = jnp.exp(arr * float(S5_T))
        t_re, t_im = tmag * jnp.cos(air * float(S5_T)), tmag * jnp.sin(air * float(S5_T))
        l_out[0, d, 0:1, 0:S5_P] = t_re
        l_out[0, d, 0:1, S5_P:2 * S5_P] = t_re
        l_out[0, d, 1:2, 0:S5_P] = -t_im
        l_out[0, d, 1:2, S5_P:2 * S5_P] = t_im
    k0_out[0] = kmats[0][:, 0:S5_HC] + kmats[1][:, 0:S5_HC]


def _s5_operators(a_re, a_im, log_dt, b_re, b_im, c_re, c_im):
    g, p, hc, t = a_re.shape[1], S5_P, S5_HC, S5_T
    nj = t * hc
    gm = lambda x: jnp.moveaxis(x, 0, 1)
    spec = lambda shape: pl.BlockSpec((1,) + shape, lambda i: (i,) + (0,) * len(shape))
    outs = pl.pallas_call(
        _s5_prep_kernel,
        grid=(g,),
        in_specs=[spec((2, p, 1)), spec((2, p, 1)), spec((2, 1, p)), spec((2, 1, p)), spec((2, 1, 1)),
                  spec((2, p, nj)), spec((2, p, nj)), spec((2, hc, p)), spec((2, hc, p)),
                  spec((2, nj, p)), spec((2, nj, p))],
        out_specs=[spec((2, p, nj)), spec((2, p, nj)), spec((2, hc, nj)), spec((hc, hc)),
                   spec((2, nj, p)), spec((2, nj, p)), spec((2, 2, 2 * p))],
        out_shape=[jax.ShapeDtypeStruct((g, 2, p, nj), F32), jax.ShapeDtypeStruct((g, 2, p, nj), F32),
                   jax.ShapeDtypeStruct((g, 2, hc, nj), F32), jax.ShapeDtypeStruct((g, hc, hc), F32),
                   jax.ShapeDtypeStruct((g, 2, nj, p), F32), jax.ShapeDtypeStruct((g, 2, nj, p), F32),
                   jax.ShapeDtypeStruct((g, 2, 2, 2 * p), F32)],
        compiler_params=_cparams(("arbitrary",)),
    )(gm(a_re)[..., None], gm(a_im)[..., None], gm(a_re)[:, :, None, :], gm(a_im)[:, :, None, :],
      gm(log_dt)[..., None, None],
      jnp.tile(gm(b_re), (1, 1, 1, t)), jnp.tile(gm(b_im), (1, 1, 1, t)), gm(c_re), gm(c_im),
      jnp.tile(gm(c_re), (1, 1, t, 1)), jnp.tile(gm(c_im), (1, 1, t, 1)))
    a_st_re, a_st_im, kmat, k0, v_re, v_im, lam_t = outs

    a5 = jnp.stack([a_st_re, a_st_im], axis=2).reshape(g, 2, 2, p, t, hc)
    a5 = jnp.stack([jnp.flip(a5[:, 0], axis=3), a5[:, 1]], axis=1)
    wcat = jnp.transpose(a5, (0, 4, 5, 1, 2, 3)).reshape(g, nj, 4 * p)
    v5 = jnp.stack([v_re, v_im], axis=2).reshape(g, 2, 2, t, hc, p)
    v5 = jnp.stack([v5[:, 0], jnp.flip(v5[:, 1], axis=2)], axis=1)
    vcat = jnp.transpose(v5, (0, 1, 2, 5, 3, 4)).reshape(g, 4 * p, nj)
    k5 = kmat.reshape(g, 2, hc, t, hc)
    ss = np.arange(t)[:, None]
    tt = np.arange(t)[None, :]
    lag = np.abs(tt - ss)
    kf = k5[:, 0][:, :, lag, :]
    kb = k5[:, 1][:, :, lag, :]
    sel_f = jnp.asarray(tt > ss)[None, None, :, :, None]
    sel_b = jnp.asarray(ss > tt)[None, None, :, :, None]
    m5 = jnp.where(sel_f, kf, jnp.where(sel_b, kb, k0[:, :, None, None, :]))
    mfb = jnp.transpose(m5, (0, 2, 4, 3, 1)).reshape(g, nj, nj)
    return wcat.astype(BF16), vcat.astype(BF16), mfb.astype(BF16), lam_t


S5_CB = TB // S5_T
S5_GH = LANE // S5_HC


def _s5_group_rows(ua_ref, ub_ref, ug_ref):
    nbat = ua_ref.shape[0]
    for t in range(S5_T):
        for half, ref in enumerate((ua_ref, ub_ref)):
            rows = jnp.concatenate([ref[bi, pl.ds(t, S5_CB, stride=S5_T), :] for bi in range(nbat)], axis=0)
            for gg in range(S5_GH):
                ug_ref[half * S5_GH + gg, :, t * S5_HC:(t + 1) * S5_HC] = rows[:, gg * S5_HC:(gg + 1) * S5_HC]


def _s5_dx_kernel(ua_ref, ub_ref, w_ref, dxf_ref, dxb_ref, ugo_ref, ug_ref):
    nbat = ua_ref.shape[0]
    _s5_group_rows(ua_ref, ub_ref, ug_ref)
    for g in range(S5_G):
        ug = ug_ref[g].astype(BF16)
        ugo_ref[0, g] = ug
        res = jnp.dot(ug, w_ref[g], preferred_element_type=F32)
        for bi in range(nbat):
            rows = slice(bi * S5_CB, (bi + 1) * S5_CB)
            dxf_ref[bi, pl.ds(g, S5_CB, stride=S5_G), :] = res[rows, 0:LANE]
            dxb_ref[bi, pl.ds(g, S5_CB, stride=S5_G), :] = res[rows, LANE:2 * LANE]


def _s5_scan_kernel(dxf_ref, dxb_ref, l_ref, xef_ref, xeb_ref, sf_ref, sb_ref):
    nbat = dxf_ref.shape[0]

    @pl.when(pl.program_id(0) == 0)
    def _():
        sf_ref[...] = jnp.zeros_like(sf_ref)
        sb_ref[...] = jnp.zeros_like(sb_ref)

    tile = lambda r: jnp.concatenate([l_ref[r]] * nbat, axis=0)
    la_f, lb_f, la_b, lb_b = tile(0), tile(1), tile(2), tile(3)

    def step(j, carry):
        xf, yf, xb, yb = carry
        rf = pl.ds(pl.multiple_of(j * S5_G, S5_G), S5_G)
        rb = pl.ds(pl.multiple_of((S5_CB - 1 - j) * S5_G, S5_G), S5_G)
        xef_ref[:, rf, :] = xf.reshape(nbat, S5_G, LANE)
        xeb_ref[:, rb, :] = xb.reshape(nbat, S5_G, LANE)
        dxf = dxf_ref[:, rf, :].reshape(nbat * S5_G, LANE)
        dxb = dxb_ref[:, rb, :].reshape(nbat * S5_G, LANE)
        xf, yf = (la_f * xf + lb_f * yf + dxf, la_f * yf - lb_f * xf + pltpu.roll(dxf, S5_P, 1))
        xb, yb = (la_b * xb + lb_b * yb + dxb, la_b * yb - lb_b * xb + pltpu.roll(dxb, S5_P, 1))
        return xf, yf, xb, yb

    xf0, xb0 = sf_ref[...], sb_ref[...]
    xf, _, xb, _ = lax.fori_loop(0, S5_CB, step,
                                 (xf0, pltpu.roll(xf0, S5_P, 1), xb0, pltpu.roll(xb0, S5_P, 1)))
    sf_ref[...] = xf
    sb_ref[...] = xb


def _s5_y_kernel(ug_ref, xef_ref, xeb_ref, m_ref, v_ref, ya_ref, yb_ref, yg_ref):
    nbat = xef_ref.shape[0]
    for g in range(S5_G):
        grows = lambda ref: jnp.concatenate(
            [ref[bi, pl.ds(g, S5_CB, stride=S5_G), :] for bi in range(nbat)], axis=0)
        xe = jnp.concatenate([grows(xef_ref), grows(xeb_ref)], axis=1).astype(BF16)
        yg_ref[g] = (jnp.dot(ug_ref[0, g], m_ref[g], preferred_element_type=F32)
                     + jnp.dot(xe, v_ref[g], preferred_element_type=F32))
    for t in range(S5_T):
        for half, ref in enumerate((ya_ref, yb_ref)):
            rows = jnp.concatenate([yg_ref[half * S5_GH + gg, :, t * S5_HC:(t + 1) * S5_HC]
                                    for gg in range(S5_GH)], axis=1)
            for bi in range(nbat):
                ref[bi, pl.ds(t, S5_CB, stride=S5_T), :] = rows[bi * S5_CB:(bi + 1) * S5_CB]


def _s5(ua, ub, wcat, vcat, mfb, lam_t):
    b, lt, _ = ua.shape
    g, nj = S5_G, S5_T * S5_HC
    nb = lt // TB
    srows = S5_CB * g
    tok = pl.BlockSpec((b, TB, LANE), lambda i: (0, i, 0))
    full = lambda a: pl.BlockSpec(a.shape, lambda i: (0,) * a.ndim)
    st_f = pl.BlockSpec((b, srows, LANE), lambda i: (0, i, 0))
    st_b = pl.BlockSpec((b, srows, LANE), lambda i: (0, _scan_block(1, i, nb), 0))
    st_shape = jax.ShapeDtypeStruct((b, (lt // S5_T) * g, LANE), F32)
    ug_spec = pl.BlockSpec((1, g, b * S5_CB, nj), lambda i: (i, 0, 0, 0))
    dxf, dxb, ug = pl.pallas_call(
        _s5_dx_kernel,
        grid=(nb,),
        in_specs=[tok, tok, full(wcat)],
        out_specs=[st_f, st_f, ug_spec],
        out_shape=[st_shape, st_shape, jax.ShapeDtypeStruct((nb, g, b * S5_CB, nj), BF16)],
        scratch_shapes=[pltpu.VMEM((g, b * S5_CB, nj), F32)],
        compiler_params=_cparams(("arbitrary",)),
    )(ua, ub, wcat)
    lam4 = jnp.moveaxis(lam_t.reshape(g, 4, LANE), 1, 0)
    xef, xeb = pl.pallas_call(
        _s5_scan_kernel,
        grid=(nb,),
        in_specs=[st_f, st_b, full(lam4)],
        out_specs=[st_f, st_b],
        out_shape=[st_shape, st_shape],
        scratch_shapes=[pltpu.VMEM((b * g, LANE), F32)] * 2,
        compiler_params=_cparams(("arbitrary",)),
    )(dxf, dxb, lam4)
    return pl.pallas_call(
        _s5_y_kernel,
        grid=(nb,),
        in_specs=[ug_spec, st_f, st_f, full(mfb), full(vcat)],
        out_specs=[tok, tok],
        out_shape=[jax.ShapeDtypeStruct((b, lt, LANE), F32)] * 2,
        scratch_shapes=[pltpu.VMEM((g, b * S5_CB, nj), F32)],
        compiler_params=_cparams(("arbitrary",)),
    )(ug, xef, xeb, mfb, vcat)


def _head_ln(o, gain):
    valid = lax.broadcasted_iota(jnp.int32, (1, LANE), 1) < GLA_DV
    outs = []
    for hh in range(GLA_H):
        oh = o[:, hh * LANE:(hh + 1) * LANE]
        mu = jnp.sum(jnp.where(valid, oh, 0.0), axis=-1, keepdims=True) * (1.0 / GLA_DV)
        oc = jnp.where(valid, oh - mu, 0.0)
        var = jnp.sum(oc * oc, axis=-1, keepdims=True) * (1.0 / GLA_DV)
        outs.append(oc * lax.rsqrt(var + LN_EPS))
    return jnp.concatenate(outs, axis=-1) * gain


OUT_ROWS = 2


def _out_kernel(alpha, hc_ref, hx_ref, mod_ref, ya_ref, yb_ref, ua_ref, ub_ref, gr_ref, mo_ref, gf_ref, gb_ref,
                mf_ref, mb_ref, dsk_ref, wglu_ref, bglu_ref, gg_ref, mg_ref, wout_ref, l1g_ref, l1b_ref,
                wup_ref, h1_ref, a_ref, v_ref, mix_ref):
    nrow = hx_ref.shape[0]
    for r in range(nrow):
        ys = jnp.concatenate([ya_ref[r], yb_ref[r]], axis=1)
        u = jnp.concatenate([ua_ref[r], ub_ref[r]], axis=1)
        y1 = jax.nn.gelu(ys + dsk_ref[...] * u)
        glu = jnp.dot(y1.astype(BF16), wglu_ref[...], preferred_element_type=F32) + bglu_ref[...]
        mix_ref[r, :, 0:S5_W] = (y1 * jax.nn.sigmoid(glu)).astype(BF16)
        gate = gr_ref[r].astype(F32)
        glo = _head_ln(gf_ref[r] + gb_ref[r], gg_ref[...])
        mix_ref[r, :, S5_W:S5_W + HPAD] = (gate * jax.nn.sigmoid(gate) * glo).astype(BF16)
        mlo = _head_ln(mf_ref[r] + mb_ref[r], mg_ref[...])
        mix_ref[r, :, S5_W + HPAD:MIX_W] = (jax.nn.sigmoid(mo_ref[r].astype(F32)) * mlo).astype(BF16)
    for r in range(nrow):
        mod = mod_ref[r, 0]
        mixed = jnp.dot(mix_ref[r], wout_ref[...], preferred_element_type=F32)
        y = alpha * _stream_block(hc_ref, hx_ref, r) + mod[2:3] * mixed
        h1 = _ln_rows(y) * l1g_ref[...] + l1b_ref[...]
        h1_ref[r] = h1
        hm = (_ln_rows(h1) * (1.0 + mod[4:5]) + mod[3:4]).astype(BF16)
        a_ref[r] = jnp.dot(hm, wup_ref[:, 0:D_FF], preferred_element_type=F32).astype(BF16)
        v_ref[r] = jnp.dot(hm, wup_ref[:, D_FF:2 * D_FF], preferred_element_type=F32).astype(BF16)


def _out_proj(alpha, h, modtab, ys5, us5, z, ogla, oml, dsk, wglu, bglu, gg, mg, wout, l1g, l1b, wup):
    d = D_MODEL
    rows = OUT_ROWS if modtab.shape[0] % OUT_ROWS == 0 else 1
    h_ops, h_specs, (b, lt) = _stream_operands(h, rows)
    nb = lt // TB
    const = lambda shape: pl.BlockSpec(shape, lambda bi, j: (0,) * len(shape), pipeline_mode=pl.Buffered(1))
    tok = lambda w, col=0: pl.BlockSpec((rows, TB, w), lambda bi, j: (bi, j, col))
    return pl.pallas_call(
        functools.partial(_out_kernel, alpha),
        grid=(b // rows, nb),
        in_specs=h_specs + [
                  pl.BlockSpec((rows, 1, 6, d), lambda bi, j: (bi, jnp.minimum(j, 1), 0, 0)),
                  tok(LANE), tok(LANE), tok(LANE), tok(LANE), tok(HPAD, Z_GR // HPAD), tok(HPAD, Z_MO // HPAD),
                  tok(HPAD), tok(HPAD), tok(HPAD), tok(HPAD),
                  const((1, S5_W)), const((S5_W, S5_W)), const((1, S5_W)), const((1, HPAD)), const((1, HPAD)),
                  const((MIX_W, d)), const((1, d)), const((1, d)), const((d, 2 * D_FF))],
        out_specs=[tok(d), tok(D_FF), tok(D_FF)],
        out_shape=[jax.ShapeDtypeStruct((b, lt, d), F32), jax.ShapeDtypeStruct((b, lt, D_FF), BF16),
                   jax.ShapeDtypeStruct((b, lt, D_FF), BF16)],
        scratch_shapes=[pltpu.VMEM((rows, TB, MIX_W), BF16)],
        compiler_params=_cparams(("arbitrary", "arbitrary")),
    )(*h_ops, modtab, ys5[0], ys5[1], us5[0], us5[1], z, z, ogla[0], ogla[1], oml[0], oml[1], dsk, wglu, bglu,
      gg, mg, wout, l1g, l1b, wup)


FF_CW = 256
GELU_E0 = -2.0 * math.sqrt(2.0 / math.pi) * math.log2(math.e)
GELU_E1 = 0.044715 * GELU_E0


def _ffn_kernel(alpha, nb, h_ref, mod_ref, a_ref, ap_ref, an_ref, v_ref, wc_ref, bc_ref, wd_ref,
                l2g_ref, l2b_ref, o_ref, g_ref):
    j = pl.program_id(1)
    is_lat = j > 0
    ngrp = TB // GRID_W
    grp = lax.broadcasted_iota(jnp.int32, (ngrp, 8, 1), 0)
    sub = lax.broadcasted_iota(jnp.int32, (ngrp, 8, 1), 1)
    edge_l = jnp.where((sub == 0) & (is_lat | (grp == 0)), 0.0, 1.0)
    edge_r = jnp.where((sub == 7) & (is_lat | (grp == ngrp - 1)), 0.0, 1.0)

    def mask_group_tile(x, first, edge):
        x = x.reshape(ngrp, GRID_W, x.shape[1])
        if first:
            x = jnp.concatenate([x[:, :8] * edge, x[:, 8:]], axis=1)
        else:
            x = jnp.concatenate([x[:, :GRID_W - 8], x[:, GRID_W - 8:] * edge], axis=1)
        return x.reshape(TB, x.shape[2])

    vert = is_lat.astype(F32)
    up_ok = (j > 1).astype(BF16)
    dn_ok = (j < nb - 1).astype(BF16)
    for r in range(h_ref.shape[0]):
        for c0 in range(0, D_FF, FF_CW):
            cs = slice(c0, c0 + FF_CW)
            a = a_ref[r, :, cs]
            up = jnp.concatenate([ap_ref[r, :, cs] * up_ok, a[:TB - GRID_W]], axis=0)
            dn = jnp.concatenate([a[GRID_W:], an_ref[r, :, cs] * dn_ok], axis=0)
            w = wc_ref[:, cs]
            wb = w.astype(BF16)
            wvb = (w * vert).astype(BF16)
            left = (up * wvb[0:1] + a * wb[3:4] + dn * wvb[6:7]).astype(F32)
            mid = (up * wvb[1:2] + a * wb[4:5] + dn * wvb[7:8]).astype(F32)
            right = (up * wvb[2:3] + a * wb[5:6] + dn * wvb[8:9]).astype(F32)
            acc = ((mid + bc_ref[:, cs]) + mask_group_tile(pltpu.roll(left, 1, 0), True, edge_l)
                   + mask_group_tile(pltpu.roll(right, TB - 1, 0), False, edge_r))
            e = jnp.exp2(acc * (GELU_E0 + GELU_E1 * (acc * acc)))
            g_ref[r, :, cs] = ((acc * v_ref[r, :, cs].astype(F32)) / (1.0 + e)).astype(BF16)
    for r in range(h_ref.shape[0]):
        f = jnp.dot(g_ref[r], wd_ref[...], preferred_element_type=F32)
        y = alpha * h_ref[r] + mod_ref[r, 0][5:6] * f
        o_ref[r] = _ln_rows(y) * l2g_ref[...] + l2b_ref[...]


def _ffn(alpha, h1, modtab, a, v, wc, bc, wd, l2g, l2b, latent_only):
    b, lt, d = h1.shape
    nb = lt // TB
    rpb = TB // GRID_W
    nrow = lt // GRID_W
    rows = OUT_ROWS if b % OUT_ROWS == 0 else 1
    const = lambda shape: pl.BlockSpec(shape, lambda bi, j: (0,) * len(shape), pipeline_mode=pl.Buffered(1))
    tok = lambda w: pl.BlockSpec((rows, TB, w), lambda bi, j: (bi, j, 0))
    if latent_only:
        out_spec = pl.BlockSpec((rows, TB, d), lambda bi, j: (bi, jnp.maximum(j - 1, 0), 0))
        out_len = lt - TB
    else:
        out_spec, out_len = tok(d), lt
    return pl.pallas_call(
        functools.partial(_ffn_kernel, alpha, nb),
        grid=(b // rows, nb),
        in_specs=[tok(d), pl.BlockSpec((rows, 1, 6, d), lambda bi, j: (bi, jnp.minimum(j, 1), 0, 0)),
                  tok(D_FF),
                  pl.BlockSpec((rows, GRID_W, D_FF), lambda bi, j: (bi, jnp.maximum(j * rpb - 1, 0), 0)),
                  pl.BlockSpec((rows, GRID_W, D_FF),
                               lambda bi, j: (bi, jnp.minimum((j + 1) * rpb, nrow - 1), 0)),
                  tok(D_FF), const((9, D_FF)), const((1, D_FF)), const((D_FF, d)),
                  const((1, d)), const((1, d))],
        out_specs=out_spec,
        out_shape=jax.ShapeDtypeStruct((b, out_len, d), F32),
        scratch_shapes=[pltpu.VMEM((rows, TB, D_FF), BF16)],
        compiler_params=_cparams(("arbitrary", "arbitrary")),
    )(h1, modtab, a, a, a, v, wc, bc, wd, l2g, l2b)


def _pad_heads(w, nh, dh):
    lead = w.shape[:-1]
    w = w.reshape(lead + (nh, dh))
    w = jnp.pad(w, [(0, 0)] * len(lead) + [(0, 0), (0, LANE - dh)])
    return w.reshape(lead + (nh * LANE,))


def _pad_head_rows(w, nh, dh):
    return jnp.moveaxis(_pad_heads(jnp.moveaxis(w, 0, -1), nh, dh), -1, 0)


def _layer_weights(w_in, gla_w_a2, gla_b_a, ml_i_bias, ml_f_bias, w_out):
    d = w_in.shape[0]
    offs = np.cumsum((0, S5_W, GLA_H * GLA_DK, GLA_H * GLA_DK, GLA_W, GLA_W, 2 * GLA_RANK,
                      ML_W, ML_W, ML_W, ML_W, 2 * ML_H, 2 * ML_H))
    part = lambda i: w_in[:, offs[i]:offs[i + 1]]
    w_p = jnp.concatenate([
        _pad_heads(part(1), GLA_H, GLA_DK), _pad_heads(part(2), GLA_H, GLA_DK),
        _pad_heads(part(3), GLA_H, GLA_DV), _pad_heads(part(4), GLA_H, GLA_DV),
        _pad_heads(part(6), ML_H, ML_D), _pad_heads(part(7), ML_H, ML_D),
        _pad_heads(part(8), ML_H, ML_D), _pad_heads(part(9), ML_H, ML_D)], axis=1).astype(BF16)
    w_u = part(0).astype(BF16)
    lr, ig, fg = part(5), part(10), part(11)
    zeros = lambda n: jnp.zeros((d, n), w_in.dtype)
    tiles = []
    for dd in range(2):
        tiles += [lr[:, dd * GLA_RANK:(dd + 1) * GLA_RANK], ig[:, dd * ML_H:(dd + 1) * ML_H],
                  zeros(LANE - GLA_RANK - ML_H),
                  zeros(GATE_OFF), fg[:, dd * ML_H:(dd + 1) * ML_H], zeros(LANE - GATE_OFF - ML_H)]
    w_g = jnp.concatenate(tiles, axis=1).astype(BF16)
    rows_t = []
    for dd in range(2):
        for gate in (ig, fg):
            rows_t += [gate[:, dd * ML_H:(dd + 1) * ML_H], zeros(8 - ML_H)]
    w_gt = jnp.concatenate(rows_t, axis=1).T.astype(BF16)
    wa = jnp.pad(_pad_heads(gla_w_a2, GLA_H, GLA_DK), ((0, 0), (0, LANE - GLA_RANK), (0, 0)))
    ba = _pad_heads(gla_b_a, GLA_H, GLA_DK)[:, None, :]
    gate_row = lambda bias: jnp.pad(bias, ((0, 0), (GATE_OFF, LANE - GATE_OFF - ML_H)))[:, None, :]
    gate_col = lambda bias: jnp.pad(bias, ((0, 0), (0, 8 - ML_H)))[:, :, None]
    wout_p = jnp.concatenate([
        w_out[:S5_W], _pad_head_rows(w_out[S5_W:S5_W + GLA_W], GLA_H, GLA_DV),
        _pad_head_rows(w_out[S5_W + GLA_W:], ML_H, ML_D)], axis=0).astype(BF16)
    return dict(w_p=w_p, w_u=w_u, w_g=w_g, w_gt=w_gt, wa=wa, ba=ba,
                ibr=gate_row(ml_i_bias), fbr=gate_row(ml_f_bias),
                ibc=gate_col(ml_i_bias), fbc=gate_col(ml_f_bias), wout=wout_p)


def kernel(x, c, ctx, c_ctx, w_ada, b_ada, w_in, s5_a_re, s5_a_im, s5_log_dt, s5_b_re, s5_b_im, s5_c_re, s5_c_im, s5_d, s5_w_glu, s5_b_glu, gla_w_a2, gla_b_a, gla_g, ml_i_bias, ml_f_bias, ml_g, w_out, ln1_g, ln1_b, w_up, w_dconv, b_dconv, w_down, ln2_g, ln2_b):
    b, seq, d = x.shape
    ctx_len = ctx.shape[1]
    depth = w_in.shape[0]
    assert d == D_MODEL and ctx_len == TB and seq % TB == 0 and b <= 7
    alpha = (2.0 * depth) ** 0.25

    cc = jnp.zeros((8, d), F32).at[:b].set(c).at[b].set(c_ctx)
    mod = _modulation(cc, w_ada, b_ada).reshape(depth, 8, 6, d)
    modtab = jnp.stack([jnp.broadcast_to(mod[:, b:b + 1], (depth, b, 6, d)), mod[:, :b]], axis=2)

    eye = np.eye(NCB, dtype=np.float32)
    low = np.kron(eye, np.tril(np.ones((CHUNK, CHUNK), np.float32)))
    tri = (jnp.asarray(low), jnp.asarray(low.T))

    fold = lambda t: jnp.moveaxis(t, 0, 1).reshape((2, depth * S5_G) + t.shape[3:])
    s5_ops = _s5_operators(*(fold(t) for t in (s5_a_re, s5_a_im, s5_log_dt, s5_b_re, s5_b_im,
                                                s5_c_re, s5_c_im)))

    h = (ctx, x)
    for l in range(depth):
        lw = _layer_weights(w_in[l], gla_w_a2[l], gla_b_a[l], ml_i_bias[l], ml_f_bias[l], w_out[l])
        z, ua, ub, gates, gates_t = _in_proj(h, modtab[l], lw["w_p"], lw["w_u"], lw["w_g"], lw["w_gt"])
        ogla = [_gla(rev, z, gates, lw["wa"], lw["ba"], tri[rev]) for rev in (0, 1)]
        oml = [_mlstm(rev, z, gates, gates_t, lw["ibr"], lw["fbr"], lw["ibc"], lw["fbc"],
                      tri[rev], tri[1 - rev]) for rev in (0, 1)]
        wcat, vcat, mfb, lam_t = (t[l * S5_G:(l + 1) * S5_G] for t in s5_ops)
        ys5 = _s5(ua, ub, wcat, vcat, mfb, lam_t)
        h1, a, v = _out_proj(alpha, h, modtab[l], ys5, (ua, ub), z, ogla, oml, s5_d[l][None],
                             s5_w_glu[l].astype(BF16),
                             s5_b_glu[l][None], _pad_heads(gla_g[l], GLA_H, GLA_DV)[None],
                             _pad_heads(ml_g[l], ML_H, ML_D)[None], lw["wout"], ln1_g[l][None],
                             ln1_b[l][None], w_up[l].astype(BF16))
        h = _ffn(alpha, h1, modtab[l], a, v, w_dconv[l].reshape(9, D_FF), b_dconv[l][None],
                 w_down[l].astype(BF16), ln2_g[l][None], ln2_b[l][None], latent_only=(l == depth - 1))
    return h
```

```python
import functools
import math

import numpy as np
import jax
import jax.numpy as jnp
from jax import lax
from jax.experimental import pallas as pl
from jax.experimental.pallas import tpu as pltpu

F32 = jnp.float32
BF16 = jnp.bfloat16
HIGHEST = lax.Precision.HIGHEST

D_MODEL = 1024
GRID_W = 64
CHUNK = 64
S5_W = D_MODEL // 4
S5_HC = 16
S5_G = S5_W // S5_HC
S5_P = 64
S5_T = 16
GLA_W = 3 * D_MODEL // 8
GLA_H = 4
GLA_DV = GLA_W // GLA_H
GLA_DK = GLA_DV // 2
GLA_RANK = 16
GLA_TAU = 16.0
ML_W = 3 * D_MODEL // 8
ML_H = 4
ML_D = ML_W // ML_H
D_FF = ((8 * D_MODEL // 3 + 127) // 128) * 128
LN_EPS = 1e-5

LANE = 128
HPAD = GLA_H * LANE
TB = 256
NCB = TB // CHUNK
GLA_PW = LANE // 2
GLA_QK = (GLA_H // 2) * LANE
Z_GQ, Z_GK, Z_GV, Z_GR = 0, GLA_QK, HPAD, 2 * HPAD
Z_MQ, Z_MK, Z_MV, Z_MO = 3 * HPAD, 4 * HPAD, 5 * HPAD, 6 * HPAD
NZM = 7 * HPAD
NGATE = 4 * LANE
NGATE_T = 32
GATE_OFF = 16
MIX_W = S5_W + 2 * HPAD

VMEM_LIMIT = 56 * 1024 * 1024


def _cparams(sem):
    return pltpu.CompilerParams(dimension_semantics=sem, vmem_limit_bytes=VMEM_LIMIT)


def _ln_rows(x):
    mu = jnp.mean(x, axis=-1, keepdims=True)
    xc = x - mu
    var = jnp.mean(xc * xc, axis=-1, keepdims=True)
    return xc * lax.rsqrt(var + LN_EPS)


def _log_sigmoid(x):
    return jnp.minimum(x, 0.0) - jnp.log(1.0 + jnp.exp(-jnp.abs(x)))


def _split_dot(tri_b, x, left):
    hi = x.astype(BF16)
    lo = (x - hi.astype(F32)).astype(BF16)
    if left:
        return (jnp.dot(tri_b, hi, preferred_element_type=F32)
                + jnp.dot(tri_b, lo, preferred_element_type=F32))
    return (jnp.dot(hi, tri_b, preferred_element_type=F32)
            + jnp.dot(lo, tri_b, preferred_element_type=F32))


_NT = (((1,), (1,)), ((), ()))
_TN = (((0,), (0,)), ((), ()))


def _mod_kernel(c_ref, w_ref, b_ref, o_ref):
    cc = c_ref[...]
    s = cc * jax.nn.sigmoid(cc)
    o_ref[0] = jnp.dot(s, w_ref[0], preferred_element_type=F32, precision=HIGHEST) + b_ref[0]


def _modulation(cc, w_ada, b_ada):
    depth, d, n = w_ada.shape
    nb = 4
    bn = n // nb
    return pl.pallas_call(
        _mod_kernel,
        grid=(depth, nb),
        in_specs=[pl.BlockSpec((8, d), lambda l, j: (0, 0)),
                  pl.BlockSpec((1, d, bn), lambda l, j: (l, 0, j)),
                  pl.BlockSpec((1, 1, bn), lambda l, j: (l, 0, j))],
        out_specs=pl.BlockSpec((1, 8, bn), lambda l, j: (l, 0, j)),
        out_shape=jax.ShapeDtypeStruct((depth, 8, n), F32),
        compiler_params=_cparams(("arbitrary", "arbitrary")),
    )(cc, w_ada, b_ada.reshape(depth, 1, n))


def _stream_operands(h, rows=1):
    if isinstance(h, tuple):
        ctx, x = h
        total = ctx.shape[1] + x.shape[1]
        run = pl.BlockSpec((rows, TB, x.shape[2]), lambda bi, j: (bi, jnp.maximum(j - 1, 0), 0))
    else:
        ctx = x = h
        total = x.shape[1]
        run = pl.BlockSpec((rows, TB, x.shape[2]), lambda bi, j: (bi, j, 0))
    head = pl.BlockSpec((rows, TB, x.shape[2]), lambda bi, j: (bi, 0, 0))
    return (ctx, x), [head, run], (x.shape[0], total)


def _stream_block(hc_ref, hx_ref, r=0):
    return jnp.where(pl.program_id(1) == 0, hc_ref[r], hx_ref[r])


IN_ROWS = 2


def _in_kernel(hc_ref, hx_ref, mod_ref, w_ref, wu_ref, wg_ref, wgt_ref, z_ref, ua_ref, ub_ref, g_ref, gt_ref):
    hms = []
    for r in range(hx_ref.shape[0]):
        mod = mod_ref[r, 0]
        hms.append((_ln_rows(_stream_block(hc_ref, hx_ref, r)) * (1.0 + mod[1:2]) + mod[0:1]).astype(BF16))
    for r, hm in enumerate(hms):
        for c0 in range(0, NZM, HPAD):
            z_ref[r, :, c0:c0 + HPAD] = jnp.dot(hm, w_ref[:, c0:c0 + HPAD],
                                                preferred_element_type=F32).astype(BF16)
        u = jnp.dot(hm, wu_ref[...], preferred_element_type=F32)
        ua_ref[r] = u[:, :LANE]
        ub_ref[r] = u[:, LANE:]
        g_ref[r] = jnp.dot(hm, wg_ref[...], preferred_element_type=F32)
        gt_ref[r] = lax.dot_general(wgt_ref[...], hm, _NT, preferred_element_type=F32)


def _in_proj(h, modtab, w_p, w_u, w_g, w_gt):
    d = D_MODEL
    rows = IN_ROWS if modtab.shape[0] % IN_ROWS == 0 else 1
    h_ops, h_specs, (b, lt) = _stream_operands(h, rows)
    nb = lt // TB
    const = lambda shape: pl.BlockSpec(shape, lambda bi, j: (0,) * len(shape))
    tok = lambda w: pl.BlockSpec((rows, TB, w), lambda bi, j: (bi, j, 0))
    return pl.pallas_call(
        _in_kernel,
        grid=(b // rows, nb),
        in_specs=h_specs + [
                  pl.BlockSpec((rows, 1, 6, d), lambda bi, j: (bi, jnp.minimum(j, 1), 0, 0)),
                  const((d, NZM)), const((d, S5_W)), const((d, NGATE)), const((NGATE_T, d))],
        out_specs=[tok(NZM), tok(LANE), tok(LANE), tok(NGATE),
                   pl.BlockSpec((rows, NGATE_T, TB), lambda bi, j: (bi, 0, j))],
        out_shape=[jax.ShapeDtypeStruct((b, lt, NZM), BF16),
                   jax.ShapeDtypeStruct((b, lt, LANE), F32),
                   jax.ShapeDtypeStruct((b, lt, LANE), F32),
                   jax.ShapeDtypeStruct((b, lt, NGATE), F32),
                   jax.ShapeDtypeStruct((b, NGATE_T, lt), F32)],
        compiler_params=_cparams(("arbitrary", "arbitrary")),
    )(*h_ops, modtab, w_p, w_u, w_g, w_gt)


def _scan_block(d, i, nb):
    return jnp.where(d == 0, i, jnp.where(i == 0, 0, nb - i))


MIXER_STREAMS = 4


def _mixer_streams(b):
    return MIXER_STREAMS if b % MIXER_STREAMS == 0 else 1


def _chunk_order(rev):
    return range(NCB - 1, -1, -1) if rev else range(NCB)


def _chunk_totals(rev, cum):
    last = 0 if rev else CHUNK - 1
    return jnp.concatenate(
        [jnp.broadcast_to(cum[c * CHUNK + last:c * CHUNK + last + 1], (CHUNK, cum.shape[1]))
         for c in range(NCB)], axis=0)


def _gla_kernel(rev, q_ref, k_ref, v_ref, g_ref, wa_ref, ba_ref, tri_ref, o_ref, st_ref):
    @pl.when(pl.program_id(1) == 0)
    def _():
        st_ref[...] = jnp.zeros_like(st_ref)

    tri = tri_ref[...]
    tri_b = tri.astype(BF16)
    causal = tri[:CHUNK, :CHUNK] > 0.0
    lane_half = lax.broadcasted_iota(jnp.int32, (1, LANE), 1) // GLA_PW
    head_lanes = [(lane_half == i).astype(BF16) for i in range(2)]
    wa_b = wa_ref[...].astype(BF16)
    for s in range(q_ref.shape[0]):
        q = q_ref[s].astype(F32) * (GLA_DK ** -0.5)
        k = k_ref[s].astype(F32)
        v = v_ref[s]
        zz = jnp.dot(g_ref[s].astype(BF16), wa_b, preferred_element_type=F32) + ba_ref[...]
        loga = _log_sigmoid(zz) * (1.0 / GLA_TAU)
        bcum = jnp.dot(tri_b, loga.astype(BF16), preferred_element_type=F32)
        btot = _chunk_totals(rev, bcum)
        qd = (q * jnp.exp(bcum)).astype(BF16)
        kd = (k * jnp.exp(-bcum)).astype(BF16)
        kdec = (k * jnp.exp(btot - bcum)).astype(BF16)
        dec = jnp.exp(btot)
        for hh in range(GLA_H):
            pair = slice((hh // 2) * LANE, (hh // 2 + 1) * LANE)
            sl = slice(hh * LANE, (hh + 1) * LANE)
            qh = qd[:, pair] * head_lanes[hh % 2]
            kh, vh, kdh = kd[:, pair], v[:, sl], kdec[:, pair]
            st = st_ref[s * GLA_H + hh]
            for c in _chunk_order(rev):
                rows = slice(c * CHUNK, (c + 1) * CHUNK)
                att = lax.dot_general(qh[rows], kh[rows], _NT, preferred_element_type=F32)
                att = jnp.where(causal, att, 0.0).astype(BF16)
                o_ref[s, rows, sl] = (jnp.dot(att, vh[rows], preferred_element_type=F32)
                                      + lax.dot_general(qh[rows], st.astype(BF16), _NT,
                                                        preferred_element_type=F32))
                st = st * dec[c * CHUNK:c * CHUNK + 1, pair] + lax.dot_general(
                    vh[rows], kdh[rows], _TN, preferred_element_type=F32)
            st_ref[s * GLA_H + hh] = st


def _gla(rev, z, gates, wa, ba, tri):
    b, lt, _ = z.shape
    nb = lt // TB
    d = int(rev)
    blk = lambda i: _scan_block(d, i, nb)
    ns = _mixer_streams(b)
    zspec = lambda w, off: pl.BlockSpec((ns, TB, w), lambda bi, i: (bi, blk(i), off // w))
    const = lambda shape: pl.BlockSpec(shape, lambda bi, i: (0,) * len(shape))
    return pl.pallas_call(
        functools.partial(_gla_kernel, rev),
        grid=(b // ns, nb),
        in_specs=[zspec(GLA_QK, Z_GQ), zspec(GLA_QK, Z_GK), zspec(HPAD, Z_GV),
                  pl.BlockSpec((ns, TB, LANE), lambda bi, i: (bi, blk(i), 2 * d)),
                  const((LANE, GLA_QK)), const((1, GLA_QK)), const((TB, TB))],
        out_specs=pl.BlockSpec((ns, TB, HPAD), lambda bi, i: (bi, blk(i), 0)),
        out_shape=jax.ShapeDtypeStruct((b, lt, HPAD), F32),
        scratch_shapes=[pltpu.VMEM((ns * GLA_H, LANE, LANE), F32)],
        compiler_params=_cparams(("arbitrary", "arbitrary")),
    )(z, z, z, gates, wa[d], ba[d], tri)


def _mlstm_kernel(rev, q_ref, k_ref, v_ref, ga_ref, gb_ref, gat_ref, gbt_ref, ibr_ref, fbr_ref,
                  ibc_ref, fbc_ref, tri_ref, trit_ref, o_ref, ct_ref, m_ref):
    @pl.when(pl.program_id(1) == 0)
    def _():
        ct_ref[...] = jnp.zeros_like(ct_ref)
        m_ref[...] = jnp.zeros_like(m_ref)

    tri = tri_ref[...]
    tri_b = tri.astype(BF16)
    trit_b = trit_ref[...].astype(BF16)
    causal = tri > 0.0
    ones_col = lax.broadcasted_iota(jnp.int32, (TB, LANE), 1) == ML_D
    for s in range(q_ref.shape[0]):
        _mlstm_stream(rev, s, tri_b, trit_b, causal, ones_col, q_ref, k_ref, v_ref, ga_ref, gb_ref,
                      gat_ref, gbt_ref, ibr_ref, fbr_ref, ibc_ref, fbc_ref, o_ref, ct_ref, m_ref)


def _mlstm_stream(rev, s, tri_b, trit_b, causal, ones_col, q_ref, k_ref, v_ref, ga_ref, gb_ref,
                  gat_ref, gbt_ref, ibr_ref, fbr_ref, ibc_ref, fbc_ref, o_ref, ct_ref, m_ref):
    q = q_ref[s]
    k = k_ref[s].astype(F32) * (ML_D ** -0.5)
    v = v_ref[s]
    igc = ga_ref[s] + ibr_ref[...]
    lfc = _log_sigmoid(gb_ref[s] + fbr_ref[...])
    igr = gat_ref[s] + ibc_ref[...]
    lfr = _log_sigmoid(gbt_ref[s] + fbc_ref[...])
    fc = _split_dot(tri_b, lfc, True)
    ftot = _chunk_totals(rev, fc)
    fr = _split_dot(trit_b, lfr, False)
    gc = ftot - fc + igc
    crow = lambda x, c: x[c * CHUNK:c * CHUNK + 1]
    m_loc = [jnp.max(gc[c * CHUNK:(c + 1) * CHUNK], axis=0, keepdims=True) for c in range(NCB)]
    m = m_ref[s]
    m_enter, a_c, b_c = [None] * NCB, [None] * NCB, [None] * NCB
    for c in _chunk_order(rev):
        m_enter[c] = m
        m_new = jnp.maximum(crow(ftot, c) + m, m_loc[c])
        a_c[c] = jnp.exp(crow(ftot, c) + m - m_new)
        b_c[c] = jnp.exp(m_loc[c] - m_new)
        m = m_new
    m_ref[s] = m
    spread = lambda rows_c: jnp.concatenate(
        [jnp.broadcast_to(r, (CHUNK, LANE)) for r in rows_c], axis=0)
    wc = jnp.exp(gc - spread(m_loc))
    key_w = igr - fr
    me_all = spread(m_enter)
    for hh in range(ML_H):
        sl = slice(hh * LANE, (hh + 1) * LANE)
        gl = slice(GATE_OFF + hh, GATE_OFF + hh + 1)
        qh = q[:, sl]
        kh = k[:, sl]
        khb = kh.astype(BF16)
        vaug = jnp.where(ones_col, 1.0, v[:, sl].astype(F32)).astype(BF16)
        qk = lax.dot_general(qh, khb, _NT, preferred_element_type=F32)
        w_log = jnp.where(causal, key_w[hh:hh + 1, :], -jnp.inf)
        me = me_all[:, gl]
        mu = jnp.maximum(me, jnp.max(w_log, axis=1, keepdims=True))
        p = (jnp.exp(w_log - mu) * qk).astype(BF16)
        nd_intra = jnp.dot(p, vaug, preferred_element_type=F32)
        w_inter = jnp.exp(me - mu)
        floor = jnp.exp(-(fc[:, gl] + mu))
        kw = (kh * wc[:, gl]).astype(BF16)
        ct = ct_ref[s * ML_H + hh]
        for c in _chunk_order(rev):
            rows = slice(c * CHUNK, (c + 1) * CHUNK)
            nd = nd_intra[rows] + w_inter[rows] * lax.dot_general(qh[rows], ct.astype(BF16), _NT,
                                                                  preferred_element_type=F32)
            den = nd[:, ML_D:ML_D + 1]
            o_ref[s, rows, sl] = nd / jnp.maximum(jnp.abs(den), floor[rows])
            ct = a_c[c][:, gl] * ct + b_c[c][:, gl] * lax.dot_general(vaug[rows], kw[rows], _TN,
                                                                      preferred_element_type=F32)
        ct_ref[s * ML_H + hh] = ct


def _mlstm(rev, z, gates, gates_t, ibr, fbr, ibc, fbc, tri, trit):
    b, lt, _ = z.shape
    nb = lt // TB
    d = int(rev)
    blk = lambda i: _scan_block(d, i, nb)
    ns = _mixer_streams(b)
    zspec = lambda col: pl.BlockSpec((ns, TB, HPAD), lambda bi, i: (bi, blk(i), col))
    gspec = lambda t: pl.BlockSpec((ns, TB, LANE), lambda bi, i: (bi, blk(i), 2 * d + t))
    gtspec = lambda t: pl.BlockSpec((ns, 8, TB), lambda bi, i: (bi, 2 * d + t, blk(i)))
    const = lambda shape: pl.BlockSpec(shape, lambda bi, i: (0,) * len(shape))
    return pl.pallas_call(
        functools.partial(_mlstm_kernel, rev),
        grid=(b // ns, nb),
        in_specs=[zspec(Z_MQ // HPAD), zspec(Z_MK // HPAD), zspec(Z_MV // HPAD),
                  gspec(0), gspec(1), gtspec(0), gtspec(1),
                  const((1, LANE)), const((1, LANE)), const((8, 1)), const((8, 1)),
                  const((TB, TB)), const((TB, TB))],
        out_specs=pl.BlockSpec((ns, TB, HPAD), lambda bi, i: (bi, blk(i), 0)),
        out_shape=jax.ShapeDtypeStruct((b, lt, HPAD), F32),
        scratch_shapes=[pltpu.VMEM((ns * ML_H, LANE, LANE), F32), pltpu.VMEM((ns, 1, LANE), F32)],
        compiler_params=_cparams(("arbitrary", "arbitrary")),
    )(z, z, z, gates, gates, gates_t, gates_t, ibr[d], fbr[d], ibc[d], fbc[d], tri, trit)


def _s5_prep_kernel(arc_ref, aic_ref, arr_ref, air_ref, ldt_ref, br_ref, bi_ref, brt_ref, bit_ref,
                    crt_ref, cit_ref, ctr_ref, cti_ref, wt_out, vt_out, m_out, l_out):
    p, hc, t = S5_P, S5_HC, S5_T
    nj = t * hc
    col_blk = lax.broadcasted_iota(jnp.int32, (1, nj), 1) // hc
    row_blk = lax.broadcasted_iota(jnp.int32, (nj, 1), 0) // hc

    def lam_pow(ar, ai, e):
        mag = jnp.exp(ar * e)
        return mag * jnp.cos(ai * e), mag * jnp.sin(ai * e)

    def zoh_gain(a_re, a_im, dt):
        lre, lim = lam_pow(a_re * dt, a_im * dt, 1.0)
        den = a_re * a_re + a_im * a_im
        return ((lre - 1.0) * a_re + lim * a_im) / den, (lim * a_re - (lre - 1.0) * a_im) / den

    def cmul(x_re, x_im, y_re, y_im):
        return x_re * y_re - x_im * y_im, x_re * y_im + x_im * y_re

    kts = []
    for d in range(2):
        dt = jnp.exp(ldt_ref[0, d])
        a_re, a_im = arc_ref[0, d], aic_ref[0, d]
        bb_re, bb_im = cmul(*zoh_gain(a_re, a_im, dt), br_ref[0, d], bi_ref[0, d])
        steps_left = (t - 1 - col_blk if d == 0 else col_blk).astype(F32)
        w_re, w_im = cmul(*lam_pow(a_re * dt, a_im * dt, steps_left), bb_re, bb_im)
        wt_out[0, 2 * d * p:(2 * d + 1) * p, :] = w_re
        wt_out[0, (2 * d + 1) * p:(2 * d + 2) * p, :] = w_im
        arr, air = arr_ref[0, d], air_ref[0, d]
        bt_re, bt_im = cmul(*zoh_gain(arr, air, dt), brt_ref[0, d], bit_ref[0, d])
        a_re_l, a_im_l = cmul(*lam_pow(arr * dt, air * dt, row_blk.astype(F32)), bt_re, bt_im)
        kts.append(jnp.dot(a_re_l, ctr_ref[0, d], preferred_element_type=F32, precision=HIGHEST)
                   - jnp.dot(a_im_l, cti_ref[0, d], preferred_element_type=F32, precision=HIGHEST))
        power = (row_blk + 1 if d == 0 else t - row_blk).astype(F32)
        v_re, v_im = cmul(*lam_pow(arr * dt, air * dt, power), crt_ref[0, d], cit_ref[0, d])
        vt_out[0, :, 2 * d * p:(2 * d + 1) * p] = v_re
        vt_out[0, :, (2 * d + 1) * p:(2 * d + 2) * p] = -v_im
        t_re, t_im = lam_pow(arr * dt, air * dt, float(t))
        l_out[0, d, 0:1, 0:p] = t_re
        l_out[0, d, 0:1, p:2 * p] = t_re
        l_out[0, d, 1:2, 0:p] = -t_im
        l_out[0, d, 1:2, p:2 * p] = t_im
    lag0 = kts[0][0:hc] + kts[1][0:hc]
    for s in range(t):
        for tt in range(t):
            lag = abs(tt - s)
            blk = lag0 if lag == 0 else (kts[0] if tt > s else kts[1])[lag * hc:(lag + 1) * hc]
            m_out[0, s * hc:(s + 1) * hc, tt * hc:(tt + 1) * hc] = blk


def _s5_operators(a_re, a_im, log_dt, b_re, b_im, c_re, c_im):
    g, p, hc, t = a_re.shape[1], S5_P, S5_HC, S5_T
    nj = t * hc
    gm = lambda x: jnp.moveaxis(x, 0, 1)
    spec = lambda shape: pl.BlockSpec((1,) + shape, lambda i: (i,) + (0,) * len(shape))
    tr = lambda x: jnp.swapaxes(x, -1, -2)
    return pl.pallas_call(
        _s5_prep_kernel,
        grid=(g,),
        in_specs=[spec((2, p, 1)), spec((2, p, 1)), spec((2, 1, p)), spec((2, 1, p)), spec((2, 1, 1)),
                  spec((2, p, nj)), spec((2, p, nj)), spec((2, nj, p)), spec((2, nj, p)),
                  spec((2, nj, p)), spec((2, nj, p)), spec((2, p, hc)), spec((2, p, hc))],
        out_specs=[spec((4 * p, nj)), spec((nj, 4 * p)), spec((nj, nj)), spec((2, 2, 2 * p))],
        out_shape=[jax.ShapeDtypeStruct((g, 4 * p, nj), F32), jax.ShapeDtypeStruct((g, nj, 4 * p), F32),
                   jax.ShapeDtypeStruct((g, nj, nj), F32), jax.ShapeDtypeStruct((g, 2, 2, 2 * p), F32)],
        compiler_params=_cparams(("arbitrary",)),
    )(gm(a_re)[..., None], gm(a_im)[..., None], gm(a_re)[:, :, None, :], gm(a_im)[:, :, None, :],
      gm(log_dt)[..., None, None],
      jnp.tile(gm(b_re), (1, 1, 1, t)), jnp.tile(gm(b_im), (1, 1, 1, t)),
      jnp.tile(tr(gm(b_re)), (1, 1, t, 1)), jnp.tile(tr(gm(b_im)), (1, 1, t, 1)),
      jnp.tile(gm(c_re), (1, 1, t, 1)), jnp.tile(gm(c_im), (1, 1, t, 1)), tr(gm(c_re)), tr(gm(c_im)))


S5_CB = TB // S5_T
S5_GH = LANE // S5_HC


def _s5_group_rows(ua_ref, ub_ref, ug_ref):
    nbat = ua_ref.shape[0]
    for t in range(S5_T):
        for half, ref in enumerate((ua_ref, ub_ref)):
            rows = jnp.concatenate([ref[bi, pl.ds(t, S5_CB, stride=S5_T), :] for bi in range(nbat)], axis=0)
            for gg in range(S5_GH):
                ug_ref[half * S5_GH + gg, :, t * S5_HC:(t + 1) * S5_HC] = rows[:, gg * S5_HC:(gg + 1) * S5_HC]


def _s5_dx_kernel(ua_ref, ub_ref, w_ref, dxf_ref, dxb_ref, ugo_ref, ug_ref):
    nbat = ua_ref.shape[0]
    _s5_group_rows(ua_ref, ub_ref, ug_ref)
    for g in range(S5_G):
        ug = ug_ref[g].astype(BF16)
        ugo_ref[0, g] = ug
        res = lax.dot_general(ug, w_ref[g].astype(BF16), _NT,
                              preferred_element_type=F32)
        for bi in range(nbat):
            rows = slice(bi * S5_CB, (bi + 1) * S5_CB)
            dxf_ref[bi, pl.ds(g, S5_CB, stride=S5_G), :] = res[rows, 0:LANE]
            dxb_ref[bi, pl.ds(g, S5_CB, stride=S5_G), :] = res[rows, LANE:2 * LANE]


def _s5_scan_kernel(dxf_ref, dxb_ref, l_ref, xef_ref, xeb_ref, sf_ref, sb_ref):
    nbat = dxf_ref.shape[0]

    @pl.when(pl.program_id(0) == 0)
    def _():
        sf_ref[...] = jnp.zeros_like(sf_ref)
        sb_ref[...] = jnp.zeros_like(sb_ref)

    tile = lambda r: jnp.concatenate([l_ref[r]] * nbat, axis=0)
    la_f, lb_f, la_b, lb_b = tile(0), tile(1), tile(2), tile(3)

    xf, xb = sf_ref[...], sb_ref[...]
    yf, yb = pltpu.roll(xf, S5_P, 1), pltpu.roll(xb, S5_P, 1)
    for j in range(S5_CB):
        rf = slice(j * S5_G, (j + 1) * S5_G)
        rb = slice((S5_CB - 1 - j) * S5_G, (S5_CB - j) * S5_G)
        xef_ref[:, rf, :] = xf.reshape(nbat, S5_G, LANE)
        xeb_ref[:, rb, :] = xb.reshape(nbat, S5_G, LANE)
        dxf = dxf_ref[:, rf, :].reshape(nbat * S5_G, LANE)
        dxb = dxb_ref[:, rb, :].reshape(nbat * S5_G, LANE)
        xf, yf = (la_f * xf + lb_f * yf + dxf, la_f * yf - lb_f * xf + pltpu.roll(dxf, S5_P, 1))
        xb, yb = (la_b * xb + lb_b * yb + dxb, la_b * yb - lb_b * xb + pltpu.roll(dxb, S5_P, 1))
    sf_ref[...] = xf
    sb_ref[...] = xb


def _s5_y_kernel(ug_ref, xef_ref, xeb_ref, m_ref, v_ref, ya_ref, yb_ref, yg_ref):
    nbat = xef_ref.shape[0]
    for g in range(S5_G):
        grows = lambda ref: jnp.concatenate(
            [ref[bi, pl.ds(g, S5_CB, stride=S5_G), :] for bi in range(nbat)], axis=0)
        xe = jnp.concatenate([grows(xef_ref), grows(xeb_ref)], axis=1).astype(BF16)
        yg_ref[g] = (jnp.dot(ug_ref[0, g], m_ref[g].astype(BF16), preferred_element_type=F32)
                     + lax.dot_general(xe, v_ref[g].astype(BF16), _NT, preferred_element_type=F32))
    for t in range(S5_T):
        for half, ref in enumerate((ya_ref, yb_ref)):
            rows = jnp.concatenate([yg_ref[half * S5_GH + gg, :, t * S5_HC:(t + 1) * S5_HC]
                                    for gg in range(S5_GH)], axis=1)
            for bi in range(nbat):
                ref[bi, pl.ds(t, S5_CB, stride=S5_T), :] = rows[bi * S5_CB:(bi + 1) * S5_CB]


def _s5(ua, ub, wt, vt, mfb, lam_t):
    b, lt, _ = ua.shape
    g, nj = S5_G, S5_T * S5_HC
    nb = lt // TB
    srows = S5_CB * g
    tok = pl.BlockSpec((b, TB, LANE), lambda i: (0, i, 0))
    full = lambda a: pl.BlockSpec(a.shape, lambda i: (0,) * a.ndim)
    st_f = pl.BlockSpec((b, srows, LANE), lambda i: (0, i, 0))
    st_b = pl.BlockSpec((b, srows, LANE), lambda i: (0, _scan_block(1, i, nb), 0))
    st_shape = jax.ShapeDtypeStruct((b, (lt // S5_T) * g, LANE), F32)
    ug_spec = pl.BlockSpec((1, g, b * S5_CB, nj), lambda i: (i, 0, 0, 0))
    dxf, dxb, ug = pl.pallas_call(
        _s5_dx_kernel,
        grid=(nb,),
        in_specs=[tok, tok, full(wt)],
        out_specs=[st_f, st_f, ug_spec],
        out_shape=[st_shape, st_shape, jax.ShapeDtypeStruct((nb, g, b * S5_CB, nj), BF16)],
        scratch_shapes=[pltpu.VMEM((g, b * S5_CB, nj), F32)],
        compiler_params=_cparams(("arbitrary",)),
    )(ua, ub, wt)
    lam4 = jnp.moveaxis(lam_t.reshape(g, 4, LANE), 1, 0)
    xef, xeb = pl.pallas_call(
        _s5_scan_kernel,
        grid=(nb,),
        in_specs=[st_f, st_b, full(lam4)],
        out_specs=[st_f, st_b],
        out_shape=[st_shape, st_shape],
        scratch_shapes=[pltpu.VMEM((b * g, LANE), F32)] * 2,
        compiler_params=_cparams(("arbitrary",)),
    )(dxf, dxb, lam4)
    return pl.pallas_call(
        _s5_y_kernel,
        grid=(nb,),
        in_specs=[ug_spec, st_f, st_f, full(mfb), full(vt)],
        out_specs=[tok, tok],
        out_shape=[jax.ShapeDtypeStruct((b, lt, LANE), F32)] * 2,
        scratch_shapes=[pltpu.VMEM((g, b * S5_CB, nj), F32)],
        compiler_params=_cparams(("arbitrary",)),
    )(ug, xef, xeb, mfb, vt)


def _head_ln(o, gain):
    valid = lax.broadcasted_iota(jnp.int32, (1, LANE), 1) < GLA_DV
    outs = []
    for hh in range(GLA_H):
        oh = o[:, hh * LANE:(hh + 1) * LANE]
        mu = jnp.sum(jnp.where(valid, oh, 0.0), axis=-1, keepdims=True) * (1.0 / GLA_DV)
        oc = jnp.where(valid, oh - mu, 0.0)
        var = jnp.sum(oc * oc, axis=-1, keepdims=True) * (1.0 / GLA_DV)
        outs.append(oc * lax.rsqrt(var + LN_EPS))
    return jnp.concatenate(outs, axis=-1) * gain


OUT_ROWS = 2


def _out_kernel(alpha, hc_ref, hx_ref, mod_ref, ya_ref, yb_ref, ua_ref, ub_ref, gr_ref, mo_ref, gf_ref, gb_ref,
                mf_ref, mb_ref, dsk_ref, wglu_ref, bglu_ref, gg_ref, mg_ref, wout_ref, l1g_ref, l1b_ref,
                wup_ref, h1_ref, a_ref, v_ref, mix_ref):
    nrow = hx_ref.shape[0]
    for r in range(nrow):
        ys = jnp.concatenate([ya_ref[r], yb_ref[r]], axis=1)
        u = jnp.concatenate([ua_ref[r], ub_ref[r]], axis=1)
        y1 = jax.nn.gelu(ys + dsk_ref[...] * u)
        glu = jnp.dot(y1.astype(BF16), wglu_ref[...], preferred_element_type=F32) + bglu_ref[...]
        mix_ref[r, :, 0:S5_W] = (y1 * jax.nn.sigmoid(glu)).astype(BF16)
        gate = gr_ref[r].astype(F32)
        glo = _head_ln(gf_ref[r] + gb_ref[r], gg_ref[...])
        mix_ref[r, :, S5_W:S5_W + HPAD] = (gate * jax.nn.sigmoid(gate) * glo).astype(BF16)
        mlo = _head_ln(mf_ref[r] + mb_ref[r], mg_ref[...])
        mix_ref[r, :, S5_W + HPAD:MIX_W] = (jax.nn.sigmoid(mo_ref[r].astype(F32)) * mlo).astype(BF16)
    for r in range(nrow):
        mod = mod_ref[r, 0]
        mixed = jnp.dot(mix_ref[r], wout_ref[...], preferred_element_type=F32)
        y = alpha * _stream_block(hc_ref, hx_ref, r) + mod[2:3] * mixed
        h1 = _ln_rows(y) * l1g_ref[...] + l1b_ref[...]
        h1_ref[r] = h1
        hm = (_ln_rows(h1) * (1.0 + mod[4:5]) + mod[3:4]).astype(BF16)
        a_ref[r] = jnp.dot(hm, wup_ref[:, 0:D_FF], preferred_element_type=F32).astype(BF16)
        v_ref[r] = jnp.dot(hm, wup_ref[:, D_FF:2 * D_FF], preferred_element_type=F32).astype(BF16)


def _out_proj(alpha, h, modtab, ys5, us5, z, ogla, oml, dsk, wglu, bglu, gg, mg, wout, l1g, l1b, wup):
    d = D_MODEL
    rows = OUT_ROWS if modtab.shape[0] % OUT_ROWS == 0 else 1
    h_ops, h_specs, (b, lt) = _stream_operands(h, rows)
    nb = lt // TB
    const = lambda shape: pl.BlockSpec(shape, lambda bi, j: (0,) * len(shape), pipeline_mode=pl.Buffered(1))
    tok = lambda w, col=0: pl.BlockSpec((rows, TB, w), lambda bi, j: (bi, j, col))
    return pl.pallas_call(
        functools.partial(_out_kernel, alpha),
        grid=(b // rows, nb),
        in_specs=h_specs + [
                  pl.BlockSpec((rows, 1, 6, d), lambda bi, j: (bi, jnp.minimum(j, 1), 0, 0)),
                  tok(LANE), tok(LANE), tok(LANE), tok(LANE), tok(HPAD, Z_GR // HPAD), tok(HPAD, Z_MO // HPAD),
                  tok(HPAD), tok(HPAD), tok(HPAD), tok(HPAD),
                  const((1, S5_W)), const((S5_W, S5_W)), const((1, S5_W)), const((1, HPAD)), const((1, HPAD)),
                  const((MIX_W, d)), const((1, d)), const((1, d)), const((d, 2 * D_FF))],
        out_specs=[tok(d), tok(D_FF), tok(D_FF)],
        out_shape=[jax.ShapeDtypeStruct((b, lt, d), F32), jax.ShapeDtypeStruct((b, lt, D_FF), BF16),
                   jax.ShapeDtypeStruct((b, lt, D_FF), BF16)],
        scratch_shapes=[pltpu.VMEM((rows, TB, MIX_W), BF16)],
        compiler_params=_cparams(("arbitrary", "arbitrary")),
    )(*h_ops, modtab, ys5[0], ys5[1], us5[0], us5[1], z, z, ogla[0], ogla[1], oml[0], oml[1], dsk, wglu, bglu,
      gg, mg, wout, l1g, l1b, wup)


FF_CW = 256
GELU_E0 = -2.0 * math.sqrt(2.0 / math.pi) * math.log2(math.e)
GELU_E1 = 0.044715 * GELU_E0


def _ffn_kernel(alpha, nb, h_ref, mod_ref, a_ref, ap_ref, an_ref, v_ref, wc_ref, bc_ref, wd_ref,
                l2g_ref, l2b_ref, o_ref, g_ref):
    j = pl.program_id(1)
    is_lat = j > 0
    ngrp = TB // GRID_W
    grp = lax.broadcasted_iota(jnp.int32, (ngrp, 8, 1), 0)
    sub = lax.broadcasted_iota(jnp.int32, (ngrp, 8, 1), 1)
    edge_l = jnp.where((sub == 0) & (is_lat | (grp == 0)), 0.0, 1.0)
    edge_r = jnp.where((sub == 7) & (is_lat | (grp == ngrp - 1)), 0.0, 1.0)

    def mask_group_tile(x, first, edge):
        x = x.reshape(ngrp, GRID_W, x.shape[1])
        if first:
            x = jnp.concatenate([x[:, :8] * edge, x[:, 8:]], axis=1)
        else:
            x = jnp.concatenate([x[:, :GRID_W - 8], x[:, GRID_W - 8:] * edge], axis=1)
        return x.reshape(TB, x.shape[2])

    vert = is_lat.astype(F32)
    up_ok = (j > 1).astype(BF16)
    dn_ok = (j < nb - 1).astype(BF16)
    for r in range(h_ref.shape[0]):
        for c0 in range(0, D_FF, FF_CW):
            cs = slice(c0, c0 + FF_CW)
            a = a_ref[r, :, cs]
            up = jnp.concatenate([ap_ref[r, :, cs] * up_ok, a[:TB - GRID_W]], axis=0)
            dn = jnp.concatenate([a[GRID_W:], an_ref[r, :, cs] * dn_ok], axis=0)
            w = wc_ref[:, cs]
            wb = w.astype(BF16)
            wvb = (w * vert).astype(BF16)
            left = (up * wvb[0:1] + a * wb[3:4] + dn * wvb[6:7]).astype(F32)
            mid = (up * wvb[1:2] + a * wb[4:5] + dn * wvb[7:8]).astype(F32)
            right = (up * wvb[2:3] + a * wb[5:6] + dn * wvb[8:9]).astype(F32)
            acc = ((mid + bc_ref[:, cs]) + mask_group_tile(pltpu.roll(left, 1, 0), True, edge_l)
                   + mask_group_tile(pltpu.roll(right, TB - 1, 0), False, edge_r))
            e = jnp.exp2(acc * (GELU_E0 + GELU_E1 * (acc * acc)))
            g_ref[r, :, cs] = ((acc * v_ref[r, :, cs].astype(F32)) / (1.0 + e)).astype(BF16)
    for r in range(h_ref.shape[0]):
        f = jnp.dot(g_ref[r], wd_ref[...], preferred_element_type=F32)
        y = alpha * h_ref[r] + mod_ref[r, 0][5:6] * f
        o_ref[r] = _ln_rows(y) * l2g_ref[...] + l2b_ref[...]


def _ffn(alpha, h1, modtab, a, v, wc, bc, wd, l2g, l2b, latent_only):
    b, lt, d = h1.shape
    nb = lt // TB
    rpb = TB // GRID_W
    nrow = lt // GRID_W
    rows = OUT_ROWS if b % OUT_ROWS == 0 else 1
    const = lambda shape: pl.BlockSpec(shape, lambda bi, j: (0,) * len(shape), pipeline_mode=pl.Buffered(1))
    tok = lambda w: pl.BlockSpec((rows, TB, w), lambda bi, j: (bi, j, 0))
    if latent_only:
        out_spec = pl.BlockSpec((rows, TB, d), lambda bi, j: (bi, jnp.maximum(j - 1, 0), 0))
        out_len = lt - TB
    else:
        out_spec, out_len = tok(d), lt
    return pl.pallas_call(
        functools.partial(_ffn_kernel, alpha, nb),
        grid=(b // rows, nb),
        in_specs=[tok(d), pl.BlockSpec((rows, 1, 6, d), lambda bi, j: (bi, jnp.minimum(j, 1), 0, 0)),
                  tok(D_FF),
                  pl.BlockSpec((rows, GRID_W, D_FF), lambda bi, j: (bi, jnp.maximum(j * rpb - 1, 0), 0)),
                  pl.BlockSpec((rows, GRID_W, D_FF),
                               lambda bi, j: (bi, jnp.minimum((j + 1) * rpb, nrow - 1), 0)),
                  tok(D_FF), const((9, D_FF)), const((1, D_FF)), const((D_FF, d)),
                  const((1, d)), const((1, d))],
        out_specs=out_spec,
        out_shape=jax.ShapeDtypeStruct((b, out_len, d), F32),
        scratch_shapes=[pltpu.VMEM((rows, TB, D_FF), BF16)],
        compiler_params=_cparams(("arbitrary", "arbitrary")),
    )(h1, modtab, a, a, a, v, wc, bc, wd, l2g, l2b)


def _pad_heads(w, nh, dh, width=LANE):
    lead = w.shape[:-1]
    w = w.reshape(lead + (nh, dh))
    w = jnp.pad(w, [(0, 0)] * len(lead) + [(0, 0), (0, width - dh)])
    return w.reshape(lead + (nh * width,))


def _pad_head_rows(w, nh, dh):
    return jnp.moveaxis(_pad_heads(jnp.moveaxis(w, 0, -1), nh, dh), -1, 0)


def _layer_weights(w_in, gla_w_a2, gla_b_a, ml_i_bias, ml_f_bias, w_out):
    d = w_in.shape[0]
    offs = np.cumsum((0, S5_W, GLA_H * GLA_DK, GLA_H * GLA_DK, GLA_W, GLA_W, 2 * GLA_RANK,
                      ML_W, ML_W, ML_W, ML_W, 2 * ML_H, 2 * ML_H))
    part = lambda i: w_in[:, offs[i]:offs[i + 1]]
    w_p = jnp.concatenate([
        _pad_heads(part(1), GLA_H, GLA_DK, GLA_PW), _pad_heads(part(2), GLA_H, GLA_DK, GLA_PW),
        _pad_heads(part(3), GLA_H, GLA_DV), _pad_heads(part(4), GLA_H, GLA_DV),
        _pad_heads(part(6), ML_H, ML_D), _pad_heads(part(7), ML_H, ML_D),
        _pad_heads(part(8), ML_H, ML_D), _pad_heads(part(9), ML_H, ML_D)], axis=1).astype(BF16)
    w_u = part(0).astype(BF16)
    lr, ig, fg = part(5), part(10), part(11)
    zeros = lambda n: jnp.zeros((d, n), w_in.dtype)
    tiles = []
    for dd in range(2):
        tiles += [lr[:, dd * GLA_RANK:(dd + 1) * GLA_RANK], ig[:, dd * ML_H:(dd + 1) * ML_H],
                  zeros(LANE - GLA_RANK - ML_H),
                  zeros(GATE_OFF), fg[:, dd * ML_H:(dd + 1) * ML_H], zeros(LANE - GATE_OFF - ML_H)]
    w_g = jnp.concatenate(tiles, axis=1).astype(BF16)
    rows_t = []
    for dd in range(2):
        for gate in (ig, fg):
            rows_t += [gate[:, dd * ML_H:(dd + 1) * ML_H], zeros(8 - ML_H)]
    w_gt = jnp.concatenate(rows_t, axis=1).T.astype(BF16)
    wa = jnp.pad(_pad_heads(gla_w_a2, GLA_H, GLA_DK, GLA_PW), ((0, 0), (0, LANE - GLA_RANK), (0, 0)))
    ba = _pad_heads(gla_b_a, GLA_H, GLA_DK, GLA_PW)[:, None, :]
    gate_row = lambda bias: jnp.pad(bias, ((0, 0), (GATE_OFF, LANE - GATE_OFF - ML_H)))[:, None, :]
    gate_col = lambda bias: jnp.pad(bias, ((0, 0), (0, 8 - ML_H)))[:, :, None]
    wout_p = jnp.concatenate([
        w_out[:S5_W], _pad_head_rows(w_out[S5_W:S5_W + GLA_W], GLA_H, GLA_DV),
        _pad_head_rows(w_out[S5_W + GLA_W:], ML_H, ML_D)], axis=0).astype(BF16)
    return dict(w_p=w_p, w_u=w_u, w_g=w_g, w_gt=w_gt, wa=wa, ba=ba,
                ibr=gate_row(ml_i_bias), fbr=gate_row(ml_f_bias),
                ibc=gate_col(ml_i_bias), fbc=gate_col(ml_f_bias), wout=wout_p)


def kernel(x, c, ctx, c_ctx, w_ada, b_ada, w_in, s5_a_re, s5_a_im, s5_log_dt, s5_b_re, s5_b_im, s5_c_re, s5_c_im, s5_d, s5_w_glu, s5_b_glu, gla_w_a2, gla_b_a, gla_g, ml_i_bias, ml_f_bias, ml_g, w_out, ln1_g, ln1_b, w_up, w_dconv, b_dconv, w_down, ln2_g, ln2_b):
    b, seq, d = x.shape
    ctx_len = ctx.shape[1]
    depth = w_in.shape[0]
    assert d == D_MODEL and ctx_len == TB and seq % TB == 0 and b <= 7
    alpha = (2.0 * depth) ** 0.25

    cc = jnp.zeros((8, d), F32).at[:b].set(c).at[b].set(c_ctx)
    mod = _modulation(cc, w_ada, b_ada).reshape(depth, 8, 6, d)
    modtab = jnp.stack([jnp.broadcast_to(mod[:, b:b + 1], (depth, b, 6, d)), mod[:, :b]], axis=2)

    eye = np.eye(NCB, dtype=np.float32)
    low = np.kron(eye, np.tril(np.ones((CHUNK, CHUNK), np.float32)))
    tri = (jnp.asarray(low), jnp.asarray(low.T))

    fold = lambda t: jnp.moveaxis(t, 0, 1).reshape((2, depth * S5_G) + t.shape[3:])
    s5_ops = _s5_operators(*(fold(t) for t in (s5_a_re, s5_a_im, s5_log_dt, s5_b_re, s5_b_im,
                                                s5_c_re, s5_c_im)))

    h = (ctx, x)
    for l in range(depth):
        lw = _layer_weights(w_in[l], gla_w_a2[l], gla_b_a[l], ml_i_bias[l], ml_f_bias[l], w_out[l])
        z, ua, ub, gates, gates_t = _in_proj(h, modtab[l], lw["w_p"], lw["w_u"], lw["w_g"], lw["w_gt"])
        ogla = [_gla(rev, z, gates, lw["wa"], lw["ba"], tri[rev]) for rev in (0, 1)]
        oml = [_mlstm(rev, z, gates, gates_t, lw["ibr"], lw["fbr"], lw["ibc"], lw["fbc"],
                      tri[rev], tri[1 - rev]) for rev in (0, 1)]
        ys5 = _s5(ua, ub, *(t[l * S5_G:(l + 1) * S5_G] for t in s5_ops))
        h1, a, v = _out_proj(alpha, h, modtab[l], ys5, (ua, ub), z, ogla, oml, s5_d[l][None],
                             s5_w_glu[l].astype(BF16),
                             s5_b_glu[l][None], _pad_heads(gla_g[l], GLA_H, GLA_DV)[None],
                             _pad_heads(ml_g[l], ML_H, ML_D)[None], lw["wout"], ln1_g[l][None],
                             ln1_b[l][None], w_up[l].astype(BF16))
        h = _ffn(alpha, h1, modtab[l], a, v, w_dconv[l].reshape(9, D_FF), b_dconv[l][None],
                 w_down[l].astype(BF16), ln2_g[l][None], ln2_b[l][None], latent_only=(l == depth - 1))
    return h
```

```python
import functools
import math

import numpy as np
import jax
import jax.numpy as jnp
from jax import lax
from jax.experimental import pallas as pl
from jax.experimental.pallas import tpu as pltpu

F32 = jnp.float32
BF16 = jnp.bfloat16
HIGHEST = lax.Precision.HIGHEST

D_MODEL = 1024
GRID_W = 64
CHUNK = 64
S5_W = D_MODEL // 4
S5_HC = 16
S5_G = S5_W // S5_HC
S5_P = 64
S5_T = 16
GLA_W = 3 * D_MODEL // 8
GLA_H = 4
GLA_DV = GLA_W // GLA_H
GLA_DK = GLA_DV // 2
GLA_RANK = 16
GLA_TAU = 16.0
ML_W = 3 * D_MODEL // 8
ML_H = 4
ML_D = ML_W // ML_H
D_FF = ((8 * D_MODEL // 3 + 127) // 128) * 128
LN_EPS = 1e-5

LANE = 128
HPAD = GLA_H * LANE
TB = 256
NCB = TB // CHUNK
GLA_PW = LANE // 2
GLA_QK = (GLA_H // 2) * LANE
Z_GQ, Z_GK, Z_GV, Z_GR = 0, GLA_QK, HPAD, 2 * HPAD
Z_MQ, Z_MK, Z_MV, Z_MO = 3 * HPAD, 4 * HPAD, 5 * HPAD, 6 * HPAD
NZM = 7 * HPAD
NGATE = 4 * LANE
NGATE_T = 32
GATE_OFF = 16
MIX_W = S5_W + 2 * HPAD

VMEM_LIMIT = 56 * 1024 * 1024


def _cparams(sem):
    return pltpu.CompilerParams(dimension_semantics=sem, vmem_limit_bytes=VMEM_LIMIT)


def _ln_rows(x):
    mu = jnp.mean(x, axis=-1, keepdims=True)
    xc = x - mu
    var = jnp.mean(xc * xc, axis=-1, keepdims=True)
    return xc * lax.rsqrt(var + LN_EPS)


def _log_sigmoid(x):
    return jnp.minimum(x, 0.0) - jnp.log(1.0 + jnp.exp(-jnp.abs(x)))


def _split_dot(tri_b, x, left):
    hi = x.astype(BF16)
    lo = (x - hi.astype(F32)).astype(BF16)
    if left:
        return (jnp.dot(tri_b, hi, preferred_element_type=F32)
                + jnp.dot(tri_b, lo, preferred_element_type=F32))
    return (jnp.dot(hi, tri_b, preferred_element_type=F32)
            + jnp.dot(lo, tri_b, preferred_element_type=F32))


_NT = (((1,), (1,)), ((), ()))
_TN = (((0,), (0,)), ((), ()))


def _mod_kernel(c_ref, w_ref, b_ref, o_ref):
    cc = c_ref[...]
    s = cc * jax.nn.sigmoid(cc)
    o_ref[0] = jnp.dot(s, w_ref[0], preferred_element_type=F32, precision=HIGHEST) + b_ref[0]


def _modulation(cc, w_ada, b_ada):
    depth, d, n = w_ada.shape
    nb = 4
    bn = n // nb
    return pl.pallas_call(
        _mod_kernel,
        grid=(depth, nb),
        in_specs=[pl.BlockSpec((8, d), lambda l, j: (0, 0)),
                  pl.BlockSpec((1, d, bn), lambda l, j: (l, 0, j)),
                  pl.BlockSpec((1, 1, bn), lambda l, j: (l, 0, j))],
        out_specs=pl.BlockSpec((1, 8, bn), lambda l, j: (l, 0, j)),
        out_shape=jax.ShapeDtypeStruct((depth, 8, n), F32),
        compiler_params=_cparams(("arbitrary", "arbitrary")),
    )(cc, w_ada, b_ada.reshape(depth, 1, n))


def _stream_operands(h, rows=1):
    if isinstance(h, tuple):
        ctx, x = h
        total = ctx.shape[1] + x.shape[1]
        run = pl.BlockSpec((rows, TB, x.shape[2]), lambda bi, j: (bi, jnp.maximum(j - 1, 0), 0))
    else:
        ctx = x = h
        total = x.shape[1]
        run = pl.BlockSpec((rows, TB, x.shape[2]), lambda bi, j: (bi, j, 0))
    head = pl.BlockSpec((rows, TB, x.shape[2]), lambda bi, j: (bi, 0, 0))
    return (ctx, x), [head, run], (x.shape[0], total)


def _stream_block(hc_ref, hx_ref, r=0):
    return jnp.where(pl.program_id(1) == 0, hc_ref[r], hx_ref[r])


IN_ROWS = 4


def _in_kernel(hc_ref, hx_ref, mod_ref, w_ref, wu_ref, wg_ref, wgt_ref, z_ref, ua_ref, ub_ref, g_ref, gt_ref):
    hms = []
    for r in range(hx_ref.shape[0]):
        mod = mod_ref[r, 0]
        hms.append((_ln_rows(_stream_block(hc_ref, hx_ref, r)) * (1.0 + mod[1:2]) + mod[0:1]).astype(BF16))
    for r, hm in enumerate(hms):
        for c0 in range(0, NZM, HPAD):
            z_ref[r, :, c0:c0 + HPAD] = jnp.dot(hm, w_ref[:, c0:c0 + HPAD],
                                                preferred_element_type=F32).astype(BF16)
        u = jnp.dot(hm, wu_ref[...], preferred_element_type=F32)
        ua_ref[r] = u[:, :LANE]
        ub_ref[r] = u[:, LANE:]
        g_ref[r] = jnp.dot(hm, wg_ref[...], preferred_element_type=F32)
        gt_ref[r] = lax.dot_general(wgt_ref[...], hm, _NT, preferred_element_type=F32)


def _in_proj(h, modtab, w_p, w_u, w_g, w_gt):
    d = D_MODEL
    rows = IN_ROWS if modtab.shape[0] % IN_ROWS == 0 else 1
    h_ops, h_specs, (b, lt) = _stream_operands(h, rows)
    nb = lt // TB
    const = lambda shape: pl.BlockSpec(shape, lambda bi, j: (0,) * len(shape), pipeline_mode=pl.Buffered(1))
    tok = lambda w: pl.BlockSpec((rows, TB, w), lambda bi, j: (bi, j, 0))
    return pl.pallas_call(
        _in_kernel,
        grid=(b // rows, nb),
        in_specs=h_specs + [
                  pl.BlockSpec((rows, 1, 6, d), lambda bi, j: (bi, jnp.minimum(j, 1), 0, 0)),
                  const((d, NZM)), const((d, S5_W)), const((d, NGATE)), const((NGATE_T, d))],
        out_specs=[tok(NZM), tok(LANE), tok(LANE), tok(NGATE),
                   pl.BlockSpec((rows, NGATE_T, TB), lambda bi, j: (bi, 0, j))],
        out_shape=[jax.ShapeDtypeStruct((b, lt, NZM), BF16),
                   jax.ShapeDtypeStruct((b, lt, LANE), F32),
                   jax.ShapeDtypeStruct((b, lt, LANE), F32),
                   jax.ShapeDtypeStruct((b, lt, NGATE), F32),
                   jax.ShapeDtypeStruct((b, NGATE_T, lt), F32)],
        compiler_params=_cparams(("arbitrary", "arbitrary")),
    )(*h_ops, modtab, w_p, w_u, w_g, w_gt)


def _scan_block(d, i, nb):
    return jnp.where(d == 0, i, jnp.where(i == 0, 0, nb - i))


MIXER_STREAMS = 4


def _mixer_streams(b):
    return MIXER_STREAMS if b % MIXER_STREAMS == 0 else 1


def _chunk_order(rev):
    return range(NCB - 1, -1, -1) if rev else range(NCB)


def _chunk_totals(rev, cum):
    last = 0 if rev else CHUNK - 1
    return jnp.concatenate(
        [jnp.broadcast_to(cum[c * CHUNK + last:c * CHUNK + last + 1], (CHUNK, cum.shape[1]))
         for c in range(NCB)], axis=0)


def _gla_kernel(rev, q_ref, k_ref, v_ref, g_ref, wa_ref, ba_ref, tri_ref, o_ref, st_ref):
    @pl.when(pl.program_id(1) == 0)
    def _():
        st_ref[...] = jnp.zeros_like(st_ref)

    tri = tri_ref[...]
    tri_b = tri.astype(BF16)
    causal = tri[:CHUNK, :CHUNK] > 0.0
    lane_half = lax.broadcasted_iota(jnp.int32, (1, LANE), 1) // GLA_PW
    head_lanes = [(lane_half == i).astype(BF16) for i in range(2)]
    wa_b = wa_ref[...].astype(BF16)
    for s in range(q_ref.shape[0]):
        q = q_ref[s].astype(F32) * (GLA_DK ** -0.5)
        k = k_ref[s].astype(F32)
        v = v_ref[s]
        zz = jnp.dot(g_ref[s].astype(BF16), wa_b, preferred_element_type=F32) + ba_ref[...]
        loga = _log_sigmoid(zz) * (1.0 / GLA_TAU)
        bcum = jnp.dot(tri_b, loga.astype(BF16), preferred_element_type=F32)
        btot = _chunk_totals(rev, bcum)
        qd = (q * jnp.exp(bcum)).astype(BF16)
        kd = (k * jnp.exp(-bcum)).astype(BF16)
        kdec = (k * jnp.exp(btot - bcum)).astype(BF16)
        dec = jnp.exp(btot)
        for hh in range(GLA_H):
            pair = slice((hh // 2) * LANE, (hh // 2 + 1) * LANE)
            sl = slice(hh * LANE, (hh + 1) * LANE)
            qh = qd[:, pair] * head_lanes[hh % 2]
            kh, vh, kdh = kd[:, pair], v[:, sl], kdec[:, pair]
            st = st_ref[s * GLA_H + hh]
            for c in _chunk_order(rev):
                rows = slice(c * CHUNK, (c + 1) * CHUNK)
                att = lax.dot_general(qh[rows], kh[rows], _NT, preferred_element_type=F32)
                att = jnp.where(causal, att, 0.0).astype(BF16)
                o_ref[s, rows, sl] = (jnp.dot(att, vh[rows], preferred_element_type=F32)
                                      + lax.dot_general(qh[rows], st.astype(BF16), _NT,
                                                        preferred_element_type=F32))
                st = st * dec[c * CHUNK:c * CHUNK + 1, pair] + lax.dot_general(
                    vh[rows], kdh[rows], _TN, preferred_element_type=F32)
            st_ref[s * GLA_H + hh] = st


def _gla(rev, z, gates, wa, ba, tri):
    b, lt, _ = z.shape
    nb = lt // TB
    d = int(rev)
    blk = lambda i: _scan_block(d, i, nb)
    ns = _mixer_streams(b)
    zspec = lambda w, off: pl.BlockSpec((ns, TB, w), lambda bi, i: (bi, blk(i), off // w))
    const = lambda shape: pl.BlockSpec(shape, lambda bi, i: (0,) * len(shape))
    return pl.pallas_call(
        functools.partial(_gla_kernel, rev),
        grid=(b // ns, nb),
        in_specs=[zspec(GLA_QK, Z_GQ), zspec(GLA_QK, Z_GK), zspec(HPAD, Z_GV),
                  pl.BlockSpec((ns, TB, LANE), lambda bi, i: (bi, blk(i), 2 * d)),
                  const((LANE, GLA_QK)), const((1, GLA_QK)), const((TB, TB))],
        out_specs=pl.BlockSpec((ns, TB, HPAD), lambda bi, i: (bi, blk(i), 0)),
        out_shape=jax.ShapeDtypeStruct((b, lt, HPAD), F32),
        scratch_shapes=[pltpu.VMEM((ns * GLA_H, LANE, LANE), F32)],
        compiler_params=_cparams(("arbitrary", "arbitrary")),
    )(z, z, z, gates, wa[d], ba[d], tri)


def _mlstm_kernel(rev, q_ref, k_ref, v_ref, ga_ref, gb_ref, gat_ref, gbt_ref, ibr_ref, fbr_ref,
                  ibc_ref, fbc_ref, tri_ref, trit_ref, o_ref, ct_ref, m_ref):
    @pl.when(pl.program_id(1) == 0)
    def _():
        ct_ref[...] = jnp.zeros_like(ct_ref)
        m_ref[...] = jnp.zeros_like(m_ref)

    tri = tri_ref[...]
    tri_b = tri.astype(BF16)
    trit_b = trit_ref[...].astype(BF16)
    causal = tri > 0.0
    ones_col = lax.broadcasted_iota(jnp.int32, (TB, LANE), 1) == ML_D
    for s in range(q_ref.shape[0]):
        _mlstm_stream(rev, s, tri_b, trit_b, causal, ones_col, q_ref, k_ref, v_ref, ga_ref, gb_ref,
                      gat_ref, gbt_ref, ibr_ref, fbr_ref, ibc_ref, fbc_ref, o_ref, ct_ref, m_ref)


def _mlstm_stream(rev, s, tri_b, trit_b, causal, ones_col, q_ref, k_ref, v_ref, ga_ref, gb_ref,
                  gat_ref, gbt_ref, ibr_ref, fbr_ref, ibc_ref, fbc_ref, o_ref, ct_ref, m_ref):
    q = q_ref[s]
    k = k_ref[s].astype(F32) * (ML_D ** -0.5)
    v = v_ref[s]
    igc = ga_ref[s] + ibr_ref[...]
    lfc = _log_sigmoid(gb_ref[s] + fbr_ref[...])
    igr = gat_ref[s] + ibc_ref[...]
    lfr = _log_sigmoid(gbt_ref[s] + fbc_ref[...])
    fc = _split_dot(tri_b, lfc, True)
    ftot = _chunk_totals(rev, fc)
    fr = _split_dot(trit_b, lfr, False)
    gc = ftot - fc + igc
    crow = lambda x, c: x[c * CHUNK:c * CHUNK + 1]
    m_loc = [jnp.max(gc[c * CHUNK:(c + 1) * CHUNK], axis=0, keepdims=True) for c in range(NCB)]
    m = m_ref[s]
    m_enter, a_c, b_c = [None] * NCB, [None] * NCB, [None] * NCB
    for c in _chunk_order(rev):
        m_enter[c] = m
        m_new = jnp.maximum(crow(ftot, c) + m, m_loc[c])
        a_c[c] = jnp.exp(crow(ftot, c) + m - m_new)
        b_c[c] = jnp.exp(m_loc[c] - m_new)
        m = m_new
    m_ref[s] = m
    spread = lambda rows_c: jnp.concatenate(
        [jnp.broadcast_to(r, (CHUNK, LANE)) for r in rows_c], axis=0)
    wc = jnp.exp(gc - spread(m_loc))
    key_w = igr - fr
    me_all = spread(m_enter)
    for hh in range(ML_H):
        sl = slice(hh * LANE, (hh + 1) * LANE)
        gl = slice(GATE_OFF + hh, GATE_OFF + hh + 1)
        qh = q[:, sl]
        kh = k[:, sl]
        khb = kh.astype(BF16)
        vaug = jnp.where(ones_col, 1.0, v[:, sl].astype(F32)).astype(BF16)
        qk = lax.dot_general(qh, khb, _NT, preferred_element_type=F32)
        w_log = jnp.where(causal, key_w[hh:hh + 1, :], -jnp.inf)
        me = me_all[:, gl]
        mu = jnp.maximum(me, jnp.max(w_log, axis=1, keepdims=True))
        p = (jnp.exp(w_log - mu) * qk).astype(BF16)
        nd_intra = jnp.dot(p, vaug, preferred_element_type=F32)
        w_inter = jnp.exp(me - mu)
        floor = jnp.exp(-(fc[:, gl] + mu))
        kw = (kh * wc[:, gl]).astype(BF16)
        ct = ct_ref[s * ML_H + hh]
        for c in _chunk_order(rev):
            rows = slice(c * CHUNK, (c + 1) * CHUNK)
            nd = nd_intra[rows] + w_inter[rows] * lax.dot_general(qh[rows], ct.astype(BF16), _NT,
                                                                  preferred_element_type=F32)
            den = nd[:, ML_D:ML_D + 1]
            o_ref[s, rows, sl] = nd / jnp.maximum(jnp.abs(den), floor[rows])
            ct = a_c[c][:, gl] * ct + b_c[c][:, gl] * lax.dot_general(vaug[rows], kw[rows], _TN,
                                                                      preferred_element_type=F32)
        ct_ref[s * ML_H + hh] = ct


def _mlstm(rev, z, gates, gates_t, ibr, fbr, ibc, fbc, tri, trit):
    b, lt, _ = z.shape
    nb = lt // TB
    d = int(rev)
    blk = lambda i: _scan_block(d, i, nb)
    ns = _mixer_streams(b)
    zspec = lambda col: pl.BlockSpec((ns, TB, HPAD), lambda bi, i: (bi, blk(i), col))
    gspec = lambda t: pl.BlockSpec((ns, TB, LANE), lambda bi, i: (bi, blk(i), 2 * d + t))
    gtspec = lambda t: pl.BlockSpec((ns, 8, TB), lambda bi, i: (bi, 2 * d + t, blk(i)))
    const = lambda shape: pl.BlockSpec(shape, lambda bi, i: (0,) * len(shape))
    return pl.pallas_call(
        functools.partial(_mlstm_kernel, rev),
        grid=(b // ns, nb),
        in_specs=[zspec(Z_MQ // HPAD), zspec(Z_MK // HPAD), zspec(Z_MV // HPAD),
                  gspec(0), gspec(1), gtspec(0), gtspec(1),
                  const((1, LANE)), const((1, LANE)), const((8, 1)), const((8, 1)),
                  const((TB, TB)), const((TB, TB))],
        out_specs=pl.BlockSpec((ns, TB, HPAD), lambda bi, i: (bi, blk(i), 0)),
        out_shape=jax.ShapeDtypeStruct((b, lt, HPAD), F32),
        scratch_shapes=[pltpu.VMEM((ns * ML_H, LANE, LANE), F32), pltpu.VMEM((ns, 1, LANE), F32)],
        compiler_params=_cparams(("arbitrary", "arbitrary")),
    )(z, z, z, gates, gates, gates_t, gates_t, ibr[d], fbr[d], ibc[d], fbc[d], tri, trit)


def _s5_prep_kernel(arc_ref, aic_ref, arr_ref, air_ref, ldt_ref, br_ref, bi_ref, brt_ref, bit_ref,
                    crt_ref, cit_ref, ctr_ref, cti_ref, wt_out, vt_out, m_out, l_out):
    p, hc, t = S5_P, S5_HC, S5_T
    nj = t * hc
    col_blk = lax.broadcasted_iota(jnp.int32, (1, nj), 1) // hc
    row_blk = lax.broadcasted_iota(jnp.int32, (nj, 1), 0) // hc

    def lam_pow(ar, ai, e):
        mag = jnp.exp(ar * e)
        return mag * jnp.cos(ai * e), mag * jnp.sin(ai * e)

    def zoh_gain(a_re, a_im, dt):
        lre, lim = lam_pow(a_re * dt, a_im * dt, 1.0)
        den = a_re * a_re + a_im * a_im
        return ((lre - 1.0) * a_re + lim * a_im) / den, (lim * a_re - (lre - 1.0) * a_im) / den

    def cmul(x_re, x_im, y_re, y_im):
        return x_re * y_re - x_im * y_im, x_re * y_im + x_im * y_re

    def powers(l_re, l_im):
        out = [(jnp.ones_like(l_re), jnp.zeros_like(l_re))]
        for _ in range(t):
            out.append(cmul(*out[-1], l_re, l_im))
        return out

    def block_table(blk, pw, power_of_block):
        re = im = 0.0
        for i in range(t):
            re = jnp.where(blk == i, pw[power_of_block(i)][0], re)
            im = jnp.where(blk == i, pw[power_of_block(i)][1], im)
        return re, im

    kts = []
    for d in range(2):
        dt = jnp.exp(ldt_ref[0, d])
        a_re, a_im = arc_ref[0, d], aic_ref[0, d]
        bb_re, bb_im = cmul(*zoh_gain(a_re, a_im, dt), br_ref[0, d], bi_ref[0, d])
        pw_col = powers(*lam_pow(a_re * dt, a_im * dt, 1.0))
        to_end = block_table(col_blk, pw_col, (lambda i: t - 1 - i) if d == 0 else (lambda i: i))
        w_re, w_im = cmul(*to_end, bb_re, bb_im)
        wt_out[0, 2 * d * p:(2 * d + 1) * p, :] = w_re
        wt_out[0, (2 * d + 1) * p:(2 * d + 2) * p, :] = w_im
        arr, air = arr_ref[0, d], air_ref[0, d]
        bt_re, bt_im = cmul(*zoh_gain(arr, air, dt), brt_ref[0, d], bit_ref[0, d])
        pw_row = powers(*lam_pow(arr * dt, air * dt, 1.0))
        a_re_l, a_im_l = cmul(*block_table(row_blk, pw_row, lambda i: i), bt_re, bt_im)
        kts.append(jnp.dot(a_re_l, ctr_ref[0, d], preferred_element_type=F32, precision=HIGHEST)
                   - jnp.dot(a_im_l, cti_ref[0, d], preferred_element_type=F32, precision=HIGHEST))
        from_start = block_table(row_blk, pw_row, (lambda i: i + 1) if d == 0 else (lambda i: t - i))
        v_re, v_im = cmul(*from_start, crt_ref[0, d], cit_ref[0, d])
        vt_out[0, :, 2 * d * p:(2 * d + 1) * p] = v_re
        vt_out[0, :, (2 * d + 1) * p:(2 * d + 2) * p] = -v_im
        t_re, t_im = pw_row[t]
        l_out[0, d, 0:1, 0:p] = t_re
        l_out[0, d, 0:1, p:2 * p] = t_re
        l_out[0, d, 1:2, 0:p] = -t_im
        l_out[0, d, 1:2, p:2 * p] = t_im
    lag0 = kts[0][0:hc] + kts[1][0:hc]
    for s in range(t):
        for tt in range(t):
            lag = abs(tt - s)
            blk = lag0 if lag == 0 else (kts[0] if tt > s else kts[1])[lag * hc:(lag + 1) * hc]
            m_out[0, s * hc:(s + 1) * hc, tt * hc:(tt + 1) * hc] = blk


def _s5_operators(a_re, a_im, log_dt, b_re, b_im, c_re, c_im):
    g, p, hc, t = a_re.shape[1], S5_P, S5_HC, S5_T
    nj = t * hc
    gm = lambda x: jnp.moveaxis(x, 0, 1)
    spec = lambda shape: pl.BlockSpec((1,) + shape, lambda i: (i,) + (0,) * len(shape))
    tr = lambda x: jnp.swapaxes(x, -1, -2)
    return pl.pallas_call(
        _s5_prep_kernel,
        grid=(g,),
        in_specs=[spec((2, p, 1)), spec((2, p, 1)), spec((2, 1, p)), spec((2, 1, p)), spec((2, 1, 1)),
                  spec((2, p, nj)), spec((2, p, nj)), spec((2, nj, p)), spec((2, nj, p)),
                  spec((2, nj, p)), spec((2, nj, p)), spec((2, p, hc)), spec((2, p, hc))],
        out_specs=[spec((4 * p, nj)), spec((nj, 4 * p)), spec((nj, nj)), spec((2, 2, 2 * p))],
        out_shape=[jax.ShapeDtypeStruct((g, 4 * p, nj), F32), jax.ShapeDtypeStruct((g, nj, 4 * p), F32),
                   jax.ShapeDtypeStruct((g, nj, nj), F32), jax.ShapeDtypeStruct((g, 2, 2, 2 * p), F32)],
        compiler_params=_cparams(("arbitrary",)),
    )(gm(a_re)[..., None], gm(a_im)[..., None], gm(a_re)[:, :, None, :], gm(a_im)[:, :, None, :],
      gm(log_dt)[..., None, None],
      jnp.tile(gm(b_re), (1, 1, 1, t)), jnp.tile(gm(b_im), (1, 1, 1, t)),
      jnp.tile(tr(gm(b_re)), (1, 1, t, 1)), jnp.tile(tr(gm(b_im)), (1, 1, t, 1)),
      jnp.tile(gm(c_re), (1, 1, t, 1)), jnp.tile(gm(c_im), (1, 1, t, 1)), tr(gm(c_re)), tr(gm(c_im)))


S5_CB = TB // S5_T
S5_GH = LANE // S5_HC


def _s5_group_rows(ua_ref, ub_ref, ug_ref):
    nbat = ua_ref.shape[0]
    for t in range(S5_T):
        for half, ref in enumerate((ua_ref, ub_ref)):
            rows = jnp.concatenate([ref[bi, pl.ds(t, S5_CB, stride=S5_T), :] for bi in range(nbat)], axis=0)
            for gg in range(S5_GH):
                ug_ref[half * S5_GH + gg, :, t * S5_HC:(t + 1) * S5_HC] = rows[:, gg * S5_HC:(gg + 1) * S5_HC]


def _s5_dx_kernel(ua_ref, ub_ref, w_ref, dxf_ref, dxb_ref, ugo_ref, ug_ref):
    nbat = ua_ref.shape[0]
    _s5_group_rows(ua_ref, ub_ref, ug_ref)
    for g in range(S5_G):
        ug = ug_ref[g].astype(BF16)
        ugo_ref[0, g] = ug
        res = lax.dot_general(ug, w_ref[g].astype(BF16), _NT,
                              preferred_element_type=F32)
        for bi in range(nbat):
            rows = slice(bi * S5_CB, (bi + 1) * S5_CB)
            dxf_ref[bi, pl.ds(g, S5_CB, stride=S5_G), :] = res[rows, 0:LANE]
            dxb_ref[bi, pl.ds(g, S5_CB, stride=S5_G), :] = res[rows, LANE:2 * LANE]


def _s5_scan_kernel(dxf_ref, dxb_ref, l_ref, xef_ref, xeb_ref, sf_ref, sb_ref):
    nbat = dxf_ref.shape[0]

    @pl.when(pl.program_id(0) == 0)
    def _():
        sf_ref[...] = jnp.zeros_like(sf_ref)
        sb_ref[...] = jnp.zeros_like(sb_ref)

    tile = lambda r: jnp.concatenate([l_ref[r]] * nbat, axis=0)
    la_f, lb_f, la_b, lb_b = tile(0), tile(1), tile(2), tile(3)

    xf, xb = sf_ref[...], sb_ref[...]
    yf, yb = pltpu.roll(xf, S5_P, 1), pltpu.roll(xb, S5_P, 1)
    for j in range(S5_CB):
        rf = slice(j * S5_G, (j + 1) * S5_G)
        rb = slice((S5_CB - 1 - j) * S5_G, (S5_CB - j) * S5_G)
        xef_ref[:, rf, :] = xf.reshape(nbat, S5_G, LANE)
        xeb_ref[:, rb, :] = xb.reshape(nbat, S5_G, LANE)
        dxf = dxf_ref[:, rf, :].reshape(nbat * S5_G, LANE)
        dxb = dxb_ref[:, rb, :].reshape(nbat * S5_G, LANE)
        xf, yf = (la_f * xf + lb_f * yf + dxf, la_f * yf - lb_f * xf + pltpu.roll(dxf, S5_P, 1))
        xb, yb = (la_b * xb + lb_b * yb + dxb, la_b * yb - lb_b * xb + pltpu.roll(dxb, S5_P, 1))
    sf_ref[...] = xf
    sb_ref[...] = xb


def _s5_y_kernel(ug_ref, xef_ref, xeb_ref, m_ref, v_ref, ya_ref, yb_ref, yg_ref):
    nbat = xef_ref.shape[0]
    for g in range(S5_G):
        grows = lambda ref: jnp.concatenate(
            [ref[bi, pl.ds(g, S5_CB, stride=S5_G), :] for bi in range(nbat)], axis=0)
        xe = jnp.concatenate([grows(xef_ref), grows(xeb_ref)], axis=1).astype(BF16)
        yg_ref[g] = (jnp.dot(ug_ref[0, g], m_ref[g].astype(BF16), preferred_element_type=F32)
                     + lax.dot_general(xe, v_ref[g].astype(BF16), _NT, preferred_element_type=F32))
    for t in range(S5_T):
        for half, ref in enumerate((ya_ref, yb_ref)):
            rows = jnp.concatenate([yg_ref[half * S5_GH + gg, :, t * S5_HC:(t + 1) * S5_HC]
                                    for gg in range(S5_GH)], axis=1)
            for bi in range(nbat):
                ref[bi, pl.ds(t, S5_CB, stride=S5_T), :] = rows[bi * S5_CB:(bi + 1) * S5_CB]


def _s5(ua, ub, wt, vt, mfb, lam_t):
    b, lt, _ = ua.shape
    g, nj = S5_G, S5_T * S5_HC
    nb = lt // TB
    srows = S5_CB * g
    tok = pl.BlockSpec((b, TB, LANE), lambda i: (0, i, 0))
    full = lambda a: pl.BlockSpec(a.shape, lambda i: (0,) * a.ndim)
    st_f = pl.BlockSpec((b, srows, LANE), lambda i: (0, i, 0))
    st_b = pl.BlockSpec((b, srows, LANE), lambda i: (0, _scan_block(1, i, nb), 0))
    st_shape = jax.ShapeDtypeStruct((b, (lt // S5_T) * g, LANE), F32)
    ug_spec = pl.BlockSpec((1, g, b * S5_CB, nj), lambda i: (i, 0, 0, 0))
    dxf, dxb, ug = pl.pallas_call(
        _s5_dx_kernel,
        grid=(nb,),
        in_specs=[tok, tok, full(wt)],
        out_specs=[st_f, st_f, ug_spec],
        out_shape=[st_shape, st_shape, jax.ShapeDtypeStruct((nb, g, b * S5_CB, nj), BF16)],
        scratch_shapes=[pltpu.VMEM((g, b * S5_CB, nj), F32)],
        compiler_params=_cparams(("arbitrary",)),
    )(ua, ub, wt)
    lam4 = jnp.moveaxis(lam_t.reshape(g, 4, LANE), 1, 0)
    xef, xeb = pl.pallas_call(
        _s5_scan_kernel,
        grid=(nb,),
        in_specs=[st_f, st_b, full(lam4)],
        out_specs=[st_f, st_b],
        out_shape=[st_shape, st_shape],
        scratch_shapes=[pltpu.VMEM((b * g, LANE), F32)] * 2,
        compiler_params=_cparams(("arbitrary",)),
    )(dxf, dxb, lam4)
    return pl.pallas_call(
        _s5_y_kernel,
        grid=(nb,),
        in_specs=[ug_spec, st_f, st_f, full(mfb), full(vt)],
        out_specs=[tok, tok],
        out_shape=[jax.ShapeDtypeStruct((b, lt, LANE), F32)] * 2,
        scratch_shapes=[pltpu.VMEM((g, b * S5_CB, nj), F32)],
        compiler_params=_cparams(("arbitrary",)),
    )(ug, xef, xeb, mfb, vt)


def _head_ln(o, gain):
    valid = lax.broadcasted_iota(jnp.int32, (1, LANE), 1) < GLA_DV
    outs = []
    for hh in range(GLA_H):
        oh = o[:, hh * LANE:(hh + 1) * LANE]
        mu = jnp.sum(jnp.where(valid, oh, 0.0), axis=-1, keepdims=True) * (1.0 / GLA_DV)
        oc = jnp.where(valid, oh - mu, 0.0)
        var = jnp.sum(oc * oc, axis=-1, keepdims=True) * (1.0 / GLA_DV)
        outs.append(oc * lax.rsqrt(var + LN_EPS))
    return jnp.concatenate(outs, axis=-1) * gain


OUT_ROWS = 2


def _out_kernel(alpha, hc_ref, hx_ref, mod_ref, ya_ref, yb_ref, ua_ref, ub_ref, gr_ref, mo_ref, gf_ref, gb_ref,
                mf_ref, mb_ref, dsk_ref, wglu_ref, bglu_ref, gg_ref, mg_ref, wout_ref, l1g_ref, l1b_ref,
                wup_ref, h1_ref, a_ref, v_ref, mix_ref):
    nrow = hx_ref.shape[0]
    for r in range(nrow):
        ys = jnp.concatenate([ya_ref[r], yb_ref[r]], axis=1)
        u = jnp.concatenate([ua_ref[r], ub_ref[r]], axis=1)
        y1 = jax.nn.gelu(ys + dsk_ref[...] * u)
        glu = jnp.dot(y1.astype(BF16), wglu_ref[...], preferred_element_type=F32) + bglu_ref[...]
        mix_ref[r, :, 0:S5_W] = (y1 * jax.nn.sigmoid(glu)).astype(BF16)
        gate = gr_ref[r].astype(F32)
        glo = _head_ln(gf_ref[r] + gb_ref[r], gg_ref[...])
        mix_ref[r, :, S5_W:S5_W + HPAD] = (gate * jax.nn.sigmoid(gate) * glo).astype(BF16)
        mlo = _head_ln(mf_ref[r] + mb_ref[r], mg_ref[...])
        mix_ref[r, :, S5_W + HPAD:MIX_W] = (jax.nn.sigmoid(mo_ref[r].astype(F32)) * mlo).astype(BF16)
    for r in range(nrow):
        mod = mod_ref[r, 0]
        mixed = jnp.dot(mix_ref[r], wout_ref[...], preferred_element_type=F32)
        y = alpha * _stream_block(hc_ref, hx_ref, r) + mod[2:3] * mixed
        h1 = _ln_rows(y) * l1g_ref[...] + l1b_ref[...]
        h1_ref[r] = h1
        hm = (_ln_rows(h1) * (1.0 + mod[4:5]) + mod[3:4]).astype(BF16)
        a_ref[r] = jnp.dot(hm, wup_ref[:, 0:D_FF], preferred_element_type=F32).astype(BF16)
        v_ref[r] = jnp.dot(hm, wup_ref[:, D_FF:2 * D_FF], preferred_element_type=F32).astype(BF16)


def _out_proj(alpha, h, modtab, ys5, us5, z, ogla, oml, dsk, wglu, bglu, gg, mg, wout, l1g, l1b, wup):
    d = D_MODEL
    rows = OUT_ROWS if modtab.shape[0] % OUT_ROWS == 0 else 1
    h_ops, h_specs, (b, lt) = _stream_operands(h, rows)
    nb = lt // TB
    const = lambda shape: pl.BlockSpec(shape, lambda bi, j: (0,) * len(shape), pipeline_mode=pl.Buffered(1))
    tok = lambda w, col=0: pl.BlockSpec((rows, TB, w), lambda bi, j: (bi, j, col))
    return pl.pallas_call(
        functools.partial(_out_kernel, alpha),
        grid=(b // rows, nb),
        in_specs=h_specs + [
                  pl.BlockSpec((rows, 1, 6, d), lambda bi, j: (bi, jnp.minimum(j, 1), 0, 0)),
                  tok(LANE), tok(LANE), tok(LANE), tok(LANE), tok(HPAD, Z_GR // HPAD), tok(HPAD, Z_MO // HPAD),
                  tok(HPAD), tok(HPAD), tok(HPAD), tok(HPAD),
                  const((1, S5_W)), const((S5_W, S5_W)), const((1, S5_W)), const((1, HPAD)), const((1, HPAD)),
                  const((MIX_W, d)), const((1, d)), const((1, d)), const((d, 2 * D_FF))],
        out_specs=[tok(d), tok(D_FF), tok(D_FF)],
        out_shape=[jax.ShapeDtypeStruct((b, lt, d), F32), jax.ShapeDtypeStruct((b, lt, D_FF), BF16),
                   jax.ShapeDtypeStruct((b, lt, D_FF), BF16)],
        scratch_shapes=[pltpu.VMEM((rows, TB, MIX_W), BF16)],
        compiler_params=_cparams(("arbitrary", "arbitrary")),
    )(*h_ops, modtab, ys5[0], ys5[1], us5[0], us5[1], z, z, ogla[0], ogla[1], oml[0], oml[1], dsk, wglu, bglu,
      gg, mg, wout, l1g, l1b, wup)


FF_CW = 256
GELU_E0 = -2.0 * math.sqrt(2.0 / math.pi) * math.log2(math.e)
GELU_E1 = 0.044715 * GELU_E0


def _ffn_kernel(alpha, nb, h_ref, mod_ref, a_ref, ap_ref, an_ref, v_ref, wc_ref, bc_ref, wd_ref,
                l2g_ref, l2b_ref, o_ref, g_ref):
    j = pl.program_id(1)
    is_lat = j > 0
    ngrp = TB // GRID_W
    grp = lax.broadcasted_iota(jnp.int32, (ngrp, 8, 1), 0)
    sub = lax.broadcasted_iota(jnp.int32, (ngrp, 8, 1), 1)
    edge_l = jnp.where((sub == 0) & (is_lat | (grp == 0)), 0.0, 1.0)
    edge_r = jnp.where((sub == 7) & (is_lat | (grp == ngrp - 1)), 0.0, 1.0)

    def mask_group_tile(x, first, edge):
        x = x.reshape(ngrp, GRID_W, x.shape[1])
        if first:
            x = jnp.concatenate([x[:, :8] * edge, x[:, 8:]], axis=1)
        else:
            x = jnp.concatenate([x[:, :GRID_W - 8], x[:, GRID_W - 8:] * edge], axis=1)
        return x.reshape(TB, x.shape[2])

    vert = is_lat.astype(F32)
    up_ok = (j > 1).astype(BF16)
    dn_ok = (j < nb - 1).astype(BF16)
    for r in range(h_ref.shape[0]):
        for c0 in range(0, D_FF, FF_CW):
            cs = slice(c0, c0 + FF_CW)
            a = a_ref[r, :, cs]
            up = jnp.concatenate([ap_ref[r, :, cs] * up_ok, a[:TB - GRID_W]], axis=0)
            dn = jnp.concatenate([a[GRID_W:], an_ref[r, :, cs] * dn_ok], axis=0)
            w = wc_ref[:, cs]
            wb = w.astype(BF16)
            wvb = (w * vert).astype(BF16)
            left = (up * wvb[0:1] + a * wb[3:4] + dn * wvb[6:7]).astype(F32)
            mid = (up * wvb[1:2] + a * wb[4:5] + dn * wvb[7:8]).astype(F32)
            right = (up * wvb[2:3] + a * wb[5:6] + dn * wvb[8:9]).astype(F32)
            acc = ((mid + bc_ref[:, cs]) + mask_group_tile(pltpu.roll(left, 1, 0), True, edge_l)
                   + mask_group_tile(pltpu.roll(right, TB - 1, 0), False, edge_r))
            e = jnp.exp2(acc * (GELU_E0 + GELU_E1 * (acc * acc)))
            g_ref[r, :, cs] = (acc / (1.0 + e)).astype(BF16) * v_ref[r, :, cs]
    for r in range(h_ref.shape[0]):
        f = jnp.dot(g_ref[r], wd_ref[...], preferred_element_type=F32)
        y = alpha * h_ref[r] + mod_ref[r, 0][5:6] * f
        o_ref[r] = _ln_rows(y) * l2g_ref[...] + l2b_ref[...]


def _ffn(alpha, h1, modtab, a, v, wc, bc, wd, l2g, l2b, latent_only):
    b, lt, d = h1.shape
    nb = lt // TB
    rpb = TB // GRID_W
    nrow = lt // GRID_W
    rows = OUT_ROWS if b % OUT_ROWS == 0 else 1
    const = lambda shape: pl.BlockSpec(shape, lambda bi, j: (0,) * len(shape), pipeline_mode=pl.Buffered(1))
    tok = lambda w: pl.BlockSpec((rows, TB, w), lambda bi, j: (bi, j, 0))
    if latent_only:
        out_spec = pl.BlockSpec((rows, TB, d), lambda bi, j: (bi, jnp.maximum(j - 1, 0), 0))
        out_len = lt - TB
    else:
        out_spec, out_len = tok(d), lt
    return pl.pallas_call(
        functools.partial(_ffn_kernel, alpha, nb),
        grid=(b // rows, nb),
        in_specs=[tok(d), pl.BlockSpec((rows, 1, 6, d), lambda bi, j: (bi, jnp.minimum(j, 1), 0, 0)),
                  tok(D_FF),
                  pl.BlockSpec((rows, GRID_W, D_FF), lambda bi, j: (bi, jnp.maximum(j * rpb - 1, 0), 0)),
                  pl.BlockSpec((rows, GRID_W, D_FF),
                               lambda bi, j: (bi, jnp.minimum((j + 1) * rpb, nrow - 1), 0)),
                  tok(D_FF), const((9, D_FF)), const((1, D_FF)), const((D_FF, d)),
                  const((1, d)), const((1, d))],
        out_specs=out_spec,
        out_shape=jax.ShapeDtypeStruct((b, out_len, d), F32),
        scratch_shapes=[pltpu.VMEM((rows, TB, D_FF), BF16)],
        compiler_params=_cparams(("arbitrary", "arbitrary")),
    )(h1, modtab, a, a, a, v, wc, bc, wd, l2g, l2b)


def _pad_heads(w, nh, dh, width=LANE):
    lead = w.shape[:-1]
    w = w.reshape(lead + (nh, dh))
    w = jnp.pad(w, [(0, 0)] * len(lead) + [(0, 0), (0, width - dh)])
    return w.reshape(lead + (nh * width,))


def _pad_head_rows(w, nh, dh):
    return jnp.moveaxis(_pad_heads(jnp.moveaxis(w, 0, -1), nh, dh), -1, 0)


def _layer_weights(w_in, gla_w_a2, gla_b_a, ml_i_bias, ml_f_bias, w_out):
    d = w_in.shape[0]
    offs = np.cumsum((0, S5_W, GLA_H * GLA_DK, GLA_H * GLA_DK, GLA_W, GLA_W, 2 * GLA_RANK,
                      ML_W, ML_W, ML_W, ML_W, 2 * ML_H, 2 * ML_H))
    part = lambda i: w_in[:, offs[i]:offs[i + 1]]
    w_p = jnp.concatenate([
        _pad_heads(part(1), GLA_H, GLA_DK, GLA_PW), _pad_heads(part(2), GLA_H, GLA_DK, GLA_PW),
        _pad_heads(part(3), GLA_H, GLA_DV), _pad_heads(part(4), GLA_H, GLA_DV),
        _pad_heads(part(6), ML_H, ML_D), _pad_heads(part(7), ML_H, ML_D),
        _pad_heads(part(8), ML_H, ML_D), _pad_heads(part(9), ML_H, ML_D)], axis=1).astype(BF16)
    w_u = part(0).astype(BF16)
    lr, ig, fg = part(5), part(10), part(11)
    zeros = lambda n: jnp.zeros((d, n), w_in.dtype)
    tiles = []
    for dd in range(2):
        tiles += [lr[:, dd * GLA_RANK:(dd + 1) * GLA_RANK], ig[:, dd * ML_H:(dd + 1) * ML_H],
                  zeros(LANE - GLA_RANK - ML_H),
                  zeros(GATE_OFF), fg[:, dd * ML_H:(dd + 1) * ML_H], zeros(LANE - GATE_OFF - ML_H)]
    w_g = jnp.concatenate(tiles, axis=1).astype(BF16)
    rows_t = []
    for dd in range(2):
        for gate in (ig, fg):
            rows_t += [gate[:, dd * ML_H:(dd + 1) * ML_H], zeros(8 - ML_H)]
    w_gt = jnp.concatenate(rows_t, axis=1).T.astype(BF16)
    wa = jnp.pad(_pad_heads(gla_w_a2, GLA_H, GLA_DK, GLA_PW), ((0, 0), (0, LANE - GLA_RANK), (0, 0)))
    ba = _pad_heads(gla_b_a, GLA_H, GLA_DK, GLA_PW)[:, None, :]
    gate_row = lambda bias: jnp.pad(bias, ((0, 0), (GATE_OFF, LANE - GATE_OFF - ML_H)))[:, None, :]
    gate_col = lambda bias: jnp.pad(bias, ((0, 0), (0, 8 - ML_H)))[:, :, None]
    wout_p = jnp.concatenate([
        w_out[:S5_W], _pad_head_rows(w_out[S5_W:S5_W + GLA_W], GLA_H, GLA_DV),
        _pad_head_rows(w_out[S5_W + GLA_W:], ML_H, ML_D)], axis=0).astype(BF16)
    return dict(w_p=w_p, w_u=w_u, w_g=w_g, w_gt=w_gt, wa=wa, ba=ba,
                ibr=gate_row(ml_i_bias), fbr=gate_row(ml_f_bias),
                ibc=gate_col(ml_i_bias), fbc=gate_col(ml_f_bias), wout=wout_p)


def kernel(x, c, ctx, c_ctx, w_ada, b_ada, w_in, s5_a_re, s5_a_im, s5_log_dt, s5_b_re, s5_b_im, s5_c_re, s5_c_im, s5_d, s5_w_glu, s5_b_glu, gla_w_a2, gla_b_a, gla_g, ml_i_bias, ml_f_bias, ml_g, w_out, ln1_g, ln1_b, w_up, w_dconv, b_dconv, w_down, ln2_g, ln2_b):
    b, seq, d = x.shape
    ctx_len = ctx.shape[1]
    depth = w_in.shape[0]
    assert d == D_MODEL and ctx_len == TB and seq % TB == 0 and b <= 7
    alpha = (2.0 * depth) ** 0.25

    cc = jnp.zeros((8, d), F32).at[:b].set(c).at[b].set(c_ctx)
    mod = _modulation(cc, w_ada, b_ada).reshape(depth, 8, 6, d)
    modtab = jnp.stack([jnp.broadcast_to(mod[:, b:b + 1], (depth, b, 6, d)), mod[:, :b]], axis=2)

    eye = np.eye(NCB, dtype=np.float32)
    low = np.kron(eye, np.tril(np.ones((CHUNK, CHUNK), np.float32)))
    tri = (jnp.asarray(low), jnp.asarray(low.T))

    fold = lambda t: jnp.moveaxis(t, 0, 1).reshape((2, depth * S5_G) + t.shape[3:])
    s5_ops = _s5_operators(*(fold(t) for t in (s5_a_re, s5_a_im, s5_log_dt, s5_b_re, s5_b_im,
                                                s5_c_re, s5_c_im)))

    h = (ctx, x)
    for l in range(depth):
        lw = _layer_weights(w_in[l], gla_w_a2[l], gla_b_a[l], ml_i_bias[l], ml_f_bias[l], w_out[l])
        z, ua, ub, gates, gates_t = _in_proj(h, modtab[l], lw["w_p"], lw["w_u"], lw["w_g"], lw["w_gt"])
        ogla = [_gla(rev, z, gates, lw["wa"], lw["ba"], tri[rev]) for rev in (0, 1)]
        oml = [_mlstm(rev, z, gates, gates_t, lw["ibr"], lw["fbr"], lw["ibc"], lw["fbc"],
                      tri[rev], tri[1 - rev]) for rev in (0, 1)]
        ys5 = _s5(ua, ub, *(t[l * S5_G:(l + 1) * S5_G] for t in s5_ops))
        h1, a, v = _out_proj(alpha, h, modtab[l], ys5, (ua, ub), z, ogla, oml, s5_d[l][None],
                             s5_w_glu[l].astype(BF16),
                             s5_b_glu[l][None], _pad_heads(gla_g[l], GLA_H, GLA_DV)[None],
                             _pad_heads(ml_g[l], ML_H, ML_D)[None], lw["wout"], ln1_g[l][None],
                             ln1_b[l][None], w_up[l].astype(BF16))
        h = _ffn(alpha, h1, modtab[l], a, v, w_dconv[l].reshape(9, D_FF), b_dconv[l][None],
                 w_down[l].astype(BF16), ln2_g[l][None], ln2_b[l][None], latent_only=(l == depth - 1))
    return h
```

```python
import functools
import math

import numpy as np
import jax
import jax.numpy as jnp
from jax import lax
from jax.experimental import pallas as pl
from jax.experimental.pallas import tpu as pltpu

F32 = jnp.float32
BF16 = jnp.bfloat16
HIGHEST = lax.Precision.HIGHEST

D_MODEL = 1024
GRID_W = 64
CHUNK = 64
S5_W = D_MODEL // 4
S5_HC = 16
S5_G = S5_W // S5_HC
S5_P = 64
S5_T = 16
GLA_W = 3 * D_MODEL // 8
GLA_H = 4
GLA_DV = GLA_W // GLA_H
GLA_DK = GLA_DV // 2
GLA_RANK = 16
GLA_TAU = 16.0
ML_W = 3 * D_MODEL // 8
ML_H = 4
ML_D = ML_W // ML_H
D_FF = ((8 * D_MODEL // 3 + 127) // 128) * 128
LN_EPS = 1e-5

LANE = 128
HPAD = GLA_H * LANE
TB = 256
NCB = TB // CHUNK
GLA_PW = LANE // 2
GLA_QK = (GLA_H // 2) * LANE
Z_GQ, Z_GK, Z_GV, Z_GR = 0, GLA_QK, HPAD, 2 * HPAD
Z_MQ, Z_MK, Z_MV, Z_MO = 3 * HPAD, 4 * HPAD, 5 * HPAD, 6 * HPAD
NZM = 7 * HPAD
NGATE = 4 * LANE
NGATE_T = 32
GATE_OFF = 16
MIX_W = S5_W + 2 * HPAD

VMEM_LIMIT = 56 * 1024 * 1024


def _cparams(sem):
    return pltpu.CompilerParams(dimension_semantics=sem, vmem_limit_bytes=VMEM_LIMIT)


def _ln_rows(x):
    mu = jnp.mean(x, axis=-1, keepdims=True)
    xc = x - mu
    var = jnp.mean(xc * xc, axis=-1, keepdims=True)
    return xc * lax.rsqrt(var + LN_EPS)


def _log_sigmoid(x):
    return jnp.minimum(x, 0.0) - jnp.log(1.0 + jnp.exp(-jnp.abs(x)))


def _split_dot(tri_b, x, left):
    hi = x.astype(BF16)
    lo = (x - hi.astype(F32)).astype(BF16)
    if left:
        return (jnp.dot(tri_b, hi, preferred_element_type=F32)
                + jnp.dot(tri_b, lo, preferred_element_type=F32))
    return (jnp.dot(hi, tri_b, preferred_element_type=F32)
            + jnp.dot(lo, tri_b, preferred_element_type=F32))


_NT = (((1,), (1,)), ((), ()))
_TN = (((0,), (0,)), ((), ()))


def _mod_kernel(c_ref, w_ref, b_ref, o_ref):
    cc = c_ref[...]
    s = cc * jax.nn.sigmoid(cc)
    o_ref[0] = jnp.dot(s, w_ref[0], preferred_element_type=F32, precision=HIGHEST) + b_ref[0]


def _modulation(cc, w_ada, b_ada):
    depth, d, n = w_ada.shape
    nb = 4
    bn = n // nb
    return pl.pallas_call(
        _mod_kernel,
        grid=(depth, nb),
        in_specs=[pl.BlockSpec((8, d), lambda l, j: (0, 0)),
                  pl.BlockSpec((1, d, bn), lambda l, j: (l, 0, j)),
                  pl.BlockSpec((1, 1, bn), lambda l, j: (l, 0, j))],
        out_specs=pl.BlockSpec((1, 8, bn), lambda l, j: (l, 0, j)),
        out_shape=jax.ShapeDtypeStruct((depth, 8, n), F32),
        compiler_params=_cparams(("arbitrary", "arbitrary")),
    )(cc, w_ada, b_ada.reshape(depth, 1, n))


def _stream_operands(h, rows=1):
    if isinstance(h, tuple):
        ctx, x = h
        total = ctx.shape[1] + x.shape[1]
        run = pl.BlockSpec((rows, TB, x.shape[2]), lambda bi, j: (bi, jnp.maximum(j - 1, 0), 0))
    else:
        ctx = x = h
        total = x.shape[1]
        run = pl.BlockSpec((rows, TB, x.shape[2]), lambda bi, j: (bi, j, 0))
    head = pl.BlockSpec((rows, TB, x.shape[2]), lambda bi, j: (bi, 0, 0))
    return (ctx, x), [head, run], (x.shape[0], total)


def _stream_block(hc_ref, hx_ref, r=0):
    return jnp.where(pl.program_id(1) == 0, hc_ref[r], hx_ref[r])


IN_ROWS = 4


def _in_kernel(hc_ref, hx_ref, mod_ref, w_ref, wu_ref, wg_ref, wgt_ref, z_ref, ua_ref, ub_ref, g_ref, gt_ref):
    hms = []
    for r in range(hx_ref.shape[0]):
        mod = mod_ref[r, 0]
        hms.append((_ln_rows(_stream_block(hc_ref, hx_ref, r)) * (1.0 + mod[1:2]) + mod[0:1]).astype(BF16))
    norm_col = lax.broadcasted_iota(jnp.int32, (1, HPAD), 1) % LANE == ML_D
    for r, hm in enumerate(hms):
        for c0 in range(0, NZM, HPAD):
            zc = jnp.dot(hm, w_ref[:, c0:c0 + HPAD], preferred_element_type=F32)
            if c0 == Z_MV:
                zc = jnp.where(norm_col, 1.0, zc)
            z_ref[r, :, c0:c0 + HPAD] = zc.astype(BF16)
        u = jnp.dot(hm, wu_ref[...], preferred_element_type=F32)
        ua_ref[r] = u[:, :LANE]
        ub_ref[r] = u[:, LANE:]
        g_ref[r] = jnp.dot(hm, wg_ref[...], preferred_element_type=F32)
        gt_ref[r] = lax.dot_general(wgt_ref[...], hm, _NT, preferred_element_type=F32)


def _in_proj(h, modtab, w_p, w_u, w_g, w_gt):
    d = D_MODEL
    rows = IN_ROWS if modtab.shape[0] % IN_ROWS == 0 else 1
    h_ops, h_specs, (b, lt) = _stream_operands(h, rows)
    nb = lt // TB
    const = lambda shape: pl.BlockSpec(shape, lambda bi, j: (0,) * len(shape), pipeline_mode=pl.Buffered(1))
    tok = lambda w: pl.BlockSpec((rows, TB, w), lambda bi, j: (bi, j, 0))
    return pl.pallas_call(
        _in_kernel,
        grid=(b // rows, nb),
        in_specs=h_specs + [
                  pl.BlockSpec((rows, 1, 6, d), lambda bi, j: (bi, jnp.minimum(j, 1), 0, 0)),
                  const((d, NZM)), const((d, S5_W)), const((d, NGATE)), const((NGATE_T, d))],
        out_specs=[tok(NZM), tok(LANE), tok(LANE), tok(NGATE),
                   pl.BlockSpec((rows, NGATE_T, TB), lambda bi, j: (bi, 0, j))],
        out_shape=[jax.ShapeDtypeStruct((b, lt, NZM), BF16),
                   jax.ShapeDtypeStruct((b, lt, LANE), F32),
                   jax.ShapeDtypeStruct((b, lt, LANE), F32),
                   jax.ShapeDtypeStruct((b, lt, NGATE), F32),
                   jax.ShapeDtypeStruct((b, NGATE_T, lt), F32)],
        compiler_params=_cparams(("arbitrary", "arbitrary")),
    )(*h_ops, modtab, w_p, w_u, w_g, w_gt)


def _scan_block(d, i, nb):
    return jnp.where(d == 0, i, jnp.where(i == 0, 0, nb - i))


MIXER_STREAMS = 4


def _mixer_streams(b):
    return MIXER_STREAMS if b % MIXER_STREAMS == 0 else 1


def _chunk_order(rev):
    return range(NCB - 1, -1, -1) if rev else range(NCB)


def _chunk_totals(rev, cum):
    last = 0 if rev else CHUNK - 1
    return jnp.concatenate(
        [jnp.broadcast_to(cum[c * CHUNK + last:c * CHUNK + last + 1], (CHUNK, cum.shape[1]))
         for c in range(NCB)], axis=0)


GLA_NIN = 7


def _gla_kernel(*refs):
    st_ref = refs[-1]

    @pl.when(pl.program_id(1) == 0)
    def _():
        st_ref[...] = jnp.zeros_like(st_ref)

    for rev in range(2):
        ins = refs[rev * GLA_NIN:(rev + 1) * GLA_NIN]
        _gla_direction(rev, *ins, refs[2 * GLA_NIN + rev], st_ref, rev * ins[0].shape[0] * GLA_H)


def _gla_direction(rev, q_ref, k_ref, v_ref, g_ref, wa_ref, ba_ref, tri_ref, o_ref, st_ref, st_base):
    tri = tri_ref[...]
    tri_b = tri.astype(BF16)
    causal = tri[:CHUNK, :CHUNK] > 0.0
    lane_half = lax.broadcasted_iota(jnp.int32, (1, LANE), 1) // GLA_PW
    head_lanes = [(lane_half == i).astype(BF16) for i in range(2)]
    wa_b = wa_ref[...].astype(BF16)
    for s in range(q_ref.shape[0]):
        q = q_ref[s].astype(F32) * (GLA_DK ** -0.5)
        k = k_ref[s].astype(F32)
        v = v_ref[s]
        zz = jnp.dot(g_ref[s].astype(BF16), wa_b, preferred_element_type=F32) + ba_ref[...]
        loga = _log_sigmoid(zz) * (1.0 / GLA_TAU)
        bcum = jnp.dot(tri_b, loga.astype(BF16), preferred_element_type=F32)
        btot = _chunk_totals(rev, bcum)
        qd = (q * jnp.exp(bcum)).astype(BF16)
        kd = (k * jnp.exp(-bcum)).astype(BF16)
        kdec = (k * jnp.exp(btot - bcum)).astype(BF16)
        dec = jnp.exp(btot)
        for hh in range(GLA_H):
            pair = slice((hh // 2) * LANE, (hh // 2 + 1) * LANE)
            sl = slice(hh * LANE, (hh + 1) * LANE)
            qh = qd[:, pair] * head_lanes[hh % 2]
            kh, vh, kdh = kd[:, pair], v[:, sl], kdec[:, pair]
            st = st_ref[st_base + s * GLA_H + hh]
            for c in _chunk_order(rev):
                rows = slice(c * CHUNK, (c + 1) * CHUNK)
                att = lax.dot_general(qh[rows], kh[rows], _NT, preferred_element_type=F32)
                att = jnp.where(causal, att, 0.0).astype(BF16)
                o_ref[s, rows, sl] = (jnp.dot(att, vh[rows], preferred_element_type=F32)
                                      + lax.dot_general(qh[rows], st.astype(BF16), _NT,
                                                        preferred_element_type=F32))
                st = st * dec[c * CHUNK:c * CHUNK + 1, pair] + lax.dot_general(
                    vh[rows], kdh[rows], _TN, preferred_element_type=F32)
            st_ref[st_base + s * GLA_H + hh] = st


def _gla(z, gates, wa, ba, tri):
    b, lt, _ = z.shape
    nb = lt // TB
    ns = _mixer_streams(b)
    const = lambda shape: pl.BlockSpec(shape, lambda bi, i: (0,) * len(shape))
    in_specs, operands, out_specs = [], [], []
    for d in range(2):
        blk = functools.partial(lambda dd, i: _scan_block(dd, i, nb), d)
        zspec = lambda w, off, blk=blk: pl.BlockSpec((ns, TB, w), lambda bi, i: (bi, blk(i), off // w))
        in_specs += [zspec(GLA_QK, Z_GQ), zspec(GLA_QK, Z_GK), zspec(HPAD, Z_GV),
                     pl.BlockSpec((ns, TB, LANE), lambda bi, i, blk=blk, d=d: (bi, blk(i), 2 * d)),
                     const((LANE, GLA_QK)), const((1, GLA_QK)), const((TB, TB))]
        operands += [z, z, z, gates, wa[d], ba[d], tri[d]]
        out_specs.append(pl.BlockSpec((ns, TB, HPAD), lambda bi, i, blk=blk: (bi, blk(i), 0)))
    return pl.pallas_call(
        _gla_kernel,
        grid=(b // ns, nb),
        in_specs=in_specs,
        out_specs=out_specs,
        out_shape=[jax.ShapeDtypeStruct((b, lt, HPAD), F32)] * 2,
        scratch_shapes=[pltpu.VMEM((2 * ns * GLA_H, LANE, LANE), F32)],
        compiler_params=_cparams(("arbitrary", "arbitrary")),
    )(*operands)


ML_NIN = 13


def _mlstm_kernel(*refs):
    ct_ref, m_ref = refs[-2:]

    @pl.when(pl.program_id(1) == 0)
    def _():
        ct_ref[...] = jnp.zeros_like(ct_ref)
        m_ref[...] = jnp.zeros_like(m_ref)

    for rev in range(2):
        ins = refs[rev * ML_NIN:(rev + 1) * ML_NIN]
        tri = ins[11][...]
        tri_b = tri.astype(BF16)
        trit_b = ins[12][...].astype(BF16)
        causal = tri > 0.0
        ns = ins[0].shape[0]
        for s in range(ns):
            _mlstm_stream(rev, s, rev * ns + s, tri_b, trit_b, causal, *ins[:11],
                          refs[2 * ML_NIN + rev], ct_ref, m_ref)


def _mlstm_stream(rev, s, slot, tri_b, trit_b, causal, q_ref, k_ref, v_ref, ga_ref, gb_ref,
                  gat_ref, gbt_ref, ibr_ref, fbr_ref, ibc_ref, fbc_ref, o_ref, ct_ref, m_ref):
    q = q_ref[s]
    k = k_ref[s].astype(F32) * (ML_D ** -0.5)
    v = v_ref[s]
    igc = ga_ref[s] + ibr_ref[...]
    lfc = _log_sigmoid(gb_ref[s] + fbr_ref[...])
    igr = gat_ref[s] + ibc_ref[...]
    lfr = _log_sigmoid(gbt_ref[s] + fbc_ref[...])
    fc = _split_dot(tri_b, lfc, True)
    ftot = _chunk_totals(rev, fc)
    fr = _split_dot(trit_b, lfr, False)
    gc = ftot - fc + igc
    crow = lambda x, c: x[c * CHUNK:c * CHUNK + 1]
    m_loc = [jnp.max(gc[c * CHUNK:(c + 1) * CHUNK], axis=0, keepdims=True) for c in range(NCB)]
    m = m_ref[slot]
    m_enter, a_c, b_c = [None] * NCB, [None] * NCB, [None] * NCB
    for c in _chunk_order(rev):
        m_enter[c] = m
        m_new = jnp.maximum(crow(ftot, c) + m, m_loc[c])
        a_c[c] = jnp.exp(crow(ftot, c) + m - m_new)
        b_c[c] = jnp.exp(m_loc[c] - m_new)
        m = m_new
    m_ref[slot] = m
    spread = lambda rows_c: jnp.concatenate(
        [jnp.broadcast_to(r, (CHUNK, LANE)) for r in rows_c], axis=0)
    wc = jnp.exp(gc - spread(m_loc))
    key_w = igr - fr
    me_all = spread(m_enter)
    for hh in range(ML_H):
        sl = slice(hh * LANE, (hh + 1) * LANE)
        gl = slice(GATE_OFF + hh, GATE_OFF + hh + 1)
        qh = q[:, sl]
        kh = k[:, sl]
        khb = kh.astype(BF16)
        vaug = v[:, sl]
        qk = lax.dot_general(qh, khb, _NT, preferred_element_type=F32)
        w_log = jnp.where(causal, key_w[hh:hh + 1, :], -jnp.inf)
        me = me_all[:, gl]
        mu = jnp.maximum(me, jnp.max(w_log, axis=1, keepdims=True))
        p = (jnp.exp(w_log - mu) * qk).astype(BF16)
        nd_intra = jnp.dot(p, vaug, preferred_element_type=F32)
        w_inter = jnp.exp(me - mu)
        floor = jnp.exp(-(fc[:, gl] + mu))
        kw = (kh * wc[:, gl]).astype(BF16)
        ct = ct_ref[slot * ML_H + hh]
        for c in _chunk_order(rev):
            rows = slice(c * CHUNK, (c + 1) * CHUNK)
            nd = nd_intra[rows] + w_inter[rows] * lax.dot_general(qh[rows], ct.astype(BF16), _NT,
                                                                  preferred_element_type=F32)
            den = nd[:, ML_D:ML_D + 1]
            o_ref[s, rows, sl] = nd / jnp.maximum(jnp.abs(den), floor[rows])
            ct = a_c[c][:, gl] * ct + b_c[c][:, gl] * lax.dot_general(vaug[rows], kw[rows], _TN,
                                                                      preferred_element_type=F32)
        ct_ref[slot * ML_H + hh] = ct


def _mlstm(z, gates, gates_t, ibr, fbr, ibc, fbc, tri):
    b, lt, _ = z.shape
    nb = lt // TB
    ns = _mixer_streams(b)
    const = lambda shape: pl.BlockSpec(shape, lambda bi, i: (0,) * len(shape))
    in_specs, operands, out_specs = [], [], []
    for d in range(2):
        blk = functools.partial(lambda dd, i: _scan_block(dd, i, nb), d)
        zspec = lambda col, blk=blk: pl.BlockSpec((ns, TB, HPAD), lambda bi, i: (bi, blk(i), col))
        gspec = lambda t, blk=blk, d=d: pl.BlockSpec((ns, TB, LANE), lambda bi, i: (bi, blk(i), 2 * d + t))
        gtspec = lambda t, blk=blk, d=d: pl.BlockSpec((ns, 8, TB), lambda bi, i: (bi, 2 * d + t, blk(i)))
        in_specs += [zspec(Z_MQ // HPAD), zspec(Z_MK // HPAD), zspec(Z_MV // HPAD),
                     gspec(0), gspec(1), gtspec(0), gtspec(1),
                     const((1, LANE)), const((1, LANE)), const((8, 1)), const((8, 1)),
                     const((TB, TB)), const((TB, TB))]
        operands += [z, z, z, gates, gates, gates_t, gates_t, ibr[d], fbr[d], ibc[d], fbc[d],
                     tri[d], tri[1 - d]]
        out_specs.append(pl.BlockSpec((ns, TB, HPAD), lambda bi, i, blk=blk: (bi, blk(i), 0)))
    return pl.pallas_call(
        _mlstm_kernel,
        grid=(b // ns, nb),
        in_specs=in_specs,
        out_specs=out_specs,
        out_shape=[jax.ShapeDtypeStruct((b, lt, HPAD), F32)] * 2,
        scratch_shapes=[pltpu.VMEM((2 * ns * ML_H, LANE, LANE), F32), pltpu.VMEM((2 * ns, 1, LANE), F32)],
        compiler_params=_cparams(("arbitrary", "arbitrary")),
    )(*operands)


def _s5_prep_kernel(arc_ref, aic_ref, arr_ref, air_ref, ldt_ref, br_ref, bi_ref, brt_ref, bit_ref,
                    crt_ref, cit_ref, ctr_ref, cti_ref, wt_out, vt_out, m_out, l_out):
    p, hc, t = S5_P, S5_HC, S5_T
    nj = t * hc
    col_blk = lax.broadcasted_iota(jnp.int32, (1, nj), 1) // hc
    row_blk = lax.broadcasted_iota(jnp.int32, (nj, 1), 0) // hc

    def lam_pow(ar, ai, e):
        mag = jnp.exp(ar * e)
        return mag * jnp.cos(ai * e), mag * jnp.sin(ai * e)

    def zoh_gain(a_re, a_im, dt):
        lre, lim = lam_pow(a_re * dt, a_im * dt, 1.0)
        den = a_re * a_re + a_im * a_im
        return ((lre - 1.0) * a_re + lim * a_im) / den, (lim * a_re - (lre - 1.0) * a_im) / den

    def cmul(x_re, x_im, y_re, y_im):
        return x_re * y_re - x_im * y_im, x_re * y_im + x_im * y_re

    def powers(l_re, l_im):
        out = [(jnp.ones_like(l_re), jnp.zeros_like(l_re))]
        for _ in range(t):
            out.append(cmul(*out[-1], l_re, l_im))
        return out

    def block_table(blk, pw, power_of_block):
        re = im = 0.0
        for i in range(t):
            re = jnp.where(blk == i, pw[power_of_block(i)][0], re)
            im = jnp.where(blk == i, pw[power_of_block(i)][1], im)
        return re, im

    kts = []
    for d in range(2):
        dt = jnp.exp(ldt_ref[0, d])
        a_re, a_im = arc_ref[0, d], aic_ref[0, d]
        bb_re, bb_im = cmul(*zoh_gain(a_re, a_im, dt), br_ref[0, d], bi_ref[0, d])
        pw_col = powers(*lam_pow(a_re * dt, a_im * dt, 1.0))
        to_end = block_table(col_blk, pw_col, (lambda i: t - 1 - i) if d == 0 else (lambda i: i))
        w_re, w_im = cmul(*to_end, bb_re, bb_im)
        wt_out[0, 2 * d * p:(2 * d + 1) * p, :] = w_re
        wt_out[0, (2 * d + 1) * p:(2 * d + 2) * p, :] = w_im
        arr, air = arr_ref[0, d], air_ref[0, d]
        bt_re, bt_im = cmul(*zoh_gain(arr, air, dt), brt_ref[0, d], bit_ref[0, d])
        pw_row = powers(*lam_pow(arr * dt, air * dt, 1.0))
        a_re_l, a_im_l = cmul(*block_table(row_blk, pw_row, lambda i: i), bt_re, bt_im)
        kts.append(jnp.dot(a_re_l, ctr_ref[0, d], preferred_element_type=F32, precision=HIGHEST)
                   - jnp.dot(a_im_l, cti_ref[0, d], preferred_element_type=F32, precision=HIGHEST))
        from_start = block_table(row_blk, pw_row, (lambda i: i + 1) if d == 0 else (lambda i: t - i))
        v_re, v_im = cmul(*from_start, crt_ref[0, d], cit_ref[0, d])
        vt_out[0, :, 2 * d * p:(2 * d + 1) * p] = v_re
        vt_out[0, :, (2 * d + 1) * p:(2 * d + 2) * p] = -v_im
        t_re, t_im = pw_row[t]
        l_out[0, d, 0:1, 0:p] = t_re
        l_out[0, d, 0:1, p:2 * p] = t_re
        l_out[0, d, 1:2, 0:p] = -t_im
        l_out[0, d, 1:2, p:2 * p] = t_im
    lag0 = kts[0][0:hc] + kts[1][0:hc]
    for s in range(t):
        for tt in range(t):
            lag = abs(tt - s)
            blk = lag0 if lag == 0 else (kts[0] if tt > s else kts[1])[lag * hc:(lag + 1) * hc]
            m_out[0, s * hc:(s + 1) * hc, tt * hc:(tt + 1) * hc] = blk


def _s5_operators(a_re, a_im, log_dt, b_re, b_im, c_re, c_im):
    g, p, hc, t = a_re.shape[1], S5_P, S5_HC, S5_T
    nj = t * hc
    gm = lambda x: jnp.moveaxis(x, 0, 1)
    spec = lambda shape: pl.BlockSpec((1,) + shape, lambda i: (i,) + (0,) * len(shape))
    tr = lambda x: jnp.swapaxes(x, -1, -2)
    return pl.pallas_call(
        _s5_prep_kernel,
        grid=(g,),
        in_specs=[spec((2, p, 1)), spec((2, p, 1)), spec((2, 1, p)), spec((2, 1, p)), spec((2, 1, 1)),
                  spec((2, p, nj)), spec((2, p, nj)), spec((2, nj, p)), spec((2, nj, p)),
                  spec((2, nj, p)), spec((2, nj, p)), spec((2, p, hc)), spec((2, p, hc))],
        out_specs=[spec((4 * p, nj)), spec((nj, 4 * p)), spec((nj, nj)), spec((2, 2, 2 * p))],
        out_shape=[jax.ShapeDtypeStruct((g, 4 * p, nj), F32), jax.ShapeDtypeStruct((g, nj, 4 * p), F32),
                   jax.ShapeDtypeStruct((g, nj, nj), F32), jax.ShapeDtypeStruct((g, 2, 2, 2 * p), F32)],
        compiler_params=_cparams(("arbitrary",)),
    )(gm(a_re)[..., None], gm(a_im)[..., None], gm(a_re)[:, :, None, :], gm(a_im)[:, :, None, :],
      gm(log_dt)[..., None, None],
      jnp.tile(gm(b_re), (1, 1, 1, t)), jnp.tile(gm(b_im), (1, 1, 1, t)),
      jnp.tile(tr(gm(b_re)), (1, 1, t, 1)), jnp.tile(tr(gm(b_im)), (1, 1, t, 1)),
      jnp.tile(gm(c_re), (1, 1, t, 1)), jnp.tile(gm(c_im), (1, 1, t, 1)), tr(gm(c_re)), tr(gm(c_im)))


S5_CB = TB // S5_T
S5_GH = LANE // S5_HC


def _s5_group_rows(ua_ref, ub_ref, ug_ref):
    nbat = ua_ref.shape[0]
    for t in range(S5_T):
        for half, ref in enumerate((ua_ref, ub_ref)):
            rows = jnp.concatenate([ref[bi, pl.ds(t, S5_CB, stride=S5_T), :] for bi in range(nbat)], axis=0)
            for gg in range(S5_GH):
                ug_ref[half * S5_GH + gg, :, t * S5_HC:(t + 1) * S5_HC] = rows[:, gg * S5_HC:(gg + 1) * S5_HC]


def _s5_dx_kernel(ua_ref, ub_ref, w_ref, dxf_ref, dxb_ref, ugo_ref, ug_ref):
    nbat = ua_ref.shape[0]
    _s5_group_rows(ua_ref, ub_ref, ug_ref)
    for g in range(S5_G):
        ug = ug_ref[g].astype(BF16)
        ugo_ref[0, g] = ug
        res = lax.dot_general(ug, w_ref[g].astype(BF16), _NT,
                              preferred_element_type=F32)
        for bi in range(nbat):
            rows = slice(bi * S5_CB, (bi + 1) * S5_CB)
            dxf_ref[bi, pl.ds(g, S5_CB, stride=S5_G), :] = res[rows, 0:LANE]
            dxb_ref[bi, pl.ds(g, S5_CB, stride=S5_G), :] = res[rows, LANE:2 * LANE]


def _s5_scan_kernel(dxf_ref, dxb_ref, l_ref, xef_ref, xeb_ref, sf_ref, sb_ref):
    nbat = dxf_ref.shape[0]

    @pl.when(pl.program_id(0) == 0)
    def _():
        sf_ref[...] = jnp.zeros_like(sf_ref)
        sb_ref[...] = jnp.zeros_like(sb_ref)

    tile = lambda r: jnp.concatenate([l_ref[r]] * nbat, axis=0)
    la_f, lb_f, la_b, lb_b = tile(0), tile(1), tile(2), tile(3)

    xf, xb = sf_ref[...], sb_ref[...]
    yf, yb = pltpu.roll(xf, S5_P, 1), pltpu.roll(xb, S5_P, 1)
    for j in range(S5_CB):
        rf = slice(j * S5_G, (j + 1) * S5_G)
        rb = slice((S5_CB - 1 - j) * S5_G, (S5_CB - j) * S5_G)
        xef_ref[:, rf, :] = xf.reshape(nbat, S5_G, LANE)
        xeb_ref[:, rb, :] = xb.reshape(nbat, S5_G, LANE)
        dxf = dxf_ref[:, rf, :].reshape(nbat * S5_G, LANE)
        dxb = dxb_ref[:, rb, :].reshape(nbat * S5_G, LANE)
        xf, yf = (la_f * xf + lb_f * yf + dxf, la_f * yf - lb_f * xf + pltpu.roll(dxf, S5_P, 1))
        xb, yb = (la_b * xb + lb_b * yb + dxb, la_b * yb - lb_b * xb + pltpu.roll(dxb, S5_P, 1))
    sf_ref[...] = xf
    sb_ref[...] = xb


def _s5_y_kernel(ug_ref, xef_ref, xeb_ref, m_ref, v_ref, ya_ref, yb_ref, yg_ref):
    nbat = xef_ref.shape[0]
    for g in range(S5_G):
        grows = lambda ref: jnp.concatenate(
            [ref[bi, pl.ds(g, S5_CB, stride=S5_G), :] for bi in range(nbat)], axis=0)
        xe = jnp.concatenate([grows(xef_ref), grows(xeb_ref)], axis=1).astype(BF16)
        yg_ref[g] = (jnp.dot(ug_ref[0, g], m_ref[g].astype(BF16), preferred_element_type=F32)
                     + lax.dot_general(xe, v_ref[g].astype(BF16), _NT, preferred_element_type=F32))
    for t in range(S5_T):
        for half, ref in enumerate((ya_ref, yb_ref)):
            rows = jnp.concatenate([yg_ref[half * S5_GH + gg, :, t * S5_HC:(t + 1) * S5_HC]
                                    for gg in range(S5_GH)], axis=1)
            for bi in range(nbat):
                ref[bi, pl.ds(t, S5_CB, stride=S5_T), :] = rows[bi * S5_CB:(bi + 1) * S5_CB]


def _s5(ua, ub, wt, vt, mfb, lam_t):
    b, lt, _ = ua.shape
    g, nj = S5_G, S5_T * S5_HC
    nb = lt // TB
    srows = S5_CB * g
    tok = pl.BlockSpec((b, TB, LANE), lambda i: (0, i, 0))
    full = lambda a: pl.BlockSpec(a.shape, lambda i: (0,) * a.ndim)
    st_f = pl.BlockSpec((b, srows, LANE), lambda i: (0, i, 0))
    st_b = pl.BlockSpec((b, srows, LANE), lambda i: (0, _scan_block(1, i, nb), 0))
    st_shape = jax.ShapeDtypeStruct((b, (lt // S5_T) * g, LANE), F32)
    ug_spec = pl.BlockSpec((1, g, b * S5_CB, nj), lambda i: (i, 0, 0, 0))
    dxf, dxb, ug = pl.pallas_call(
        _s5_dx_kernel,
        grid=(nb,),
        in_specs=[tok, tok, full(wt)],
        out_specs=[st_f, st_f, ug_spec],
        out_shape=[st_shape, st_shape, jax.ShapeDtypeStruct((nb, g, b * S5_CB, nj), BF16)],
        scratch_shapes=[pltpu.VMEM((g, b * S5_CB, nj), F32)],
        compiler_params=_cparams(("arbitrary",)),
    )(ua, ub, wt)
    lam4 = jnp.moveaxis(lam_t.reshape(g, 4, LANE), 1, 0)
    xef, xeb = pl.pallas_call(
        _s5_scan_kernel,
        grid=(nb,),
        in_specs=[st_f, st_b, full(lam4)],
        out_specs=[st_f, st_b],
        out_shape=[st_shape, st_shape],
        scratch_shapes=[pltpu.VMEM((b * g, LANE), F32)] * 2,
        compiler_params=_cparams(("arbitrary",)),
    )(dxf, dxb, lam4)
    return pl.pallas_call(
        _s5_y_kernel,
        grid=(nb,),
        in_specs=[ug_spec, st_f, st_f, full(mfb), full(vt)],
        out_specs=[tok, tok],
        out_shape=[jax.ShapeDtypeStruct((b, lt, LANE), F32)] * 2,
        scratch_shapes=[pltpu.VMEM((g, b * S5_CB, nj), F32)],
        compiler_params=_cparams(("arbitrary",)),
    )(ug, xef, xeb, mfb, vt)


def _head_ln(o, gain):
    valid = lax.broadcasted_iota(jnp.int32, (1, LANE), 1) < GLA_DV
    outs = []
    for hh in range(GLA_H):
        oh = o[:, hh * LANE:(hh + 1) * LANE]
        mu = jnp.sum(jnp.where(valid, oh, 0.0), axis=-1, keepdims=True) * (1.0 / GLA_DV)
        oc = jnp.where(valid, oh - mu, 0.0)
        var = jnp.sum(oc * oc, axis=-1, keepdims=True) * (1.0 / GLA_DV)
        outs.append(oc * lax.rsqrt(var + LN_EPS))
    return jnp.concatenate(outs, axis=-1) * gain


OUT_ROWS = 2


def _out_kernel(alpha, hc_ref, hx_ref, mod_ref, ya_ref, yb_ref, ua_ref, ub_ref, gr_ref, mo_ref, gf_ref, gb_ref,
                mf_ref, mb_ref, dsk_ref, wglu_ref, bglu_ref, gg_ref, mg_ref, wout_ref, l1g_ref, l1b_ref,
                wup_ref, h1_ref, a_ref, v_ref, mix_ref):
    nrow = hx_ref.shape[0]
    for r in range(nrow):
        ys = jnp.concatenate([ya_ref[r], yb_ref[r]], axis=1)
        u = jnp.concatenate([ua_ref[r], ub_ref[r]], axis=1)
        y1 = jax.nn.gelu(ys + dsk_ref[...] * u)
        glu = jnp.dot(y1.astype(BF16), wglu_ref[...], preferred_element_type=F32) + bglu_ref[...]
        mix_ref[r, :, 0:S5_W] = (y1 * jax.nn.sigmoid(glu)).astype(BF16)
        gate = gr_ref[r].astype(F32)
        glo = _head_ln(gf_ref[r] + gb_ref[r], gg_ref[...])
        mix_ref[r, :, S5_W:S5_W + HPAD] = (gate * jax.nn.sigmoid(gate) * glo).astype(BF16)
        mlo = _head_ln(mf_ref[r] + mb_ref[r], mg_ref[...])
        mix_ref[r, :, S5_W + HPAD:MIX_W] = (jax.nn.sigmoid(mo_ref[r].astype(F32)) * mlo).astype(BF16)
    for r in range(nrow):
        mod = mod_ref[r, 0]
        mixed = jnp.dot(mix_ref[r], wout_ref[...], preferred_element_type=F32)
        y = alpha * _stream_block(hc_ref, hx_ref, r) + mod[2:3] * mixed
        h1 = _ln_rows(y) * l1g_ref[...] + l1b_ref[...]
        h1_ref[r] = h1
        hm = (_ln_rows(h1) * (1.0 + mod[4:5]) + mod[3:4]).astype(BF16)
        a_ref[r] = jnp.dot(hm, wup_ref[:, 0:D_FF], preferred_element_type=F32).astype(BF16)
        v_ref[r] = jnp.dot(hm, wup_ref[:, D_FF:2 * D_FF], preferred_element_type=F32).astype(BF16)


def _out_proj(alpha, h, modtab, ys5, us5, z, ogla, oml, dsk, wglu, bglu, gg, mg, wout, l1g, l1b, wup):
    d = D_MODEL
    rows = OUT_ROWS if modtab.shape[0] % OUT_ROWS == 0 else 1
    h_ops, h_specs, (b, lt) = _stream_operands(h, rows)
    nb = lt // TB
    const = lambda shape: pl.BlockSpec(shape, lambda bi, j: (0,) * len(shape), pipeline_mode=pl.Buffered(1))
    tok = lambda w, col=0: pl.BlockSpec((rows, TB, w), lambda bi, j: (bi, j, col))
    return pl.pallas_call(
        functools.partial(_out_kernel, alpha),
        grid=(b // rows, nb),
        in_specs=h_specs + [
                  pl.BlockSpec((rows, 1, 6, d), lambda bi, j: (bi, jnp.minimum(j, 1), 0, 0)),
                  tok(LANE), tok(LANE), tok(LANE), tok(LANE), tok(HPAD, Z_GR // HPAD), tok(HPAD, Z_MO // HPAD),
                  tok(HPAD), tok(HPAD), tok(HPAD), tok(HPAD),
                  const((1, S5_W)), const((S5_W, S5_W)), const((1, S5_W)), const((1, HPAD)), const((1, HPAD)),
                  const((MIX_W, d)), const((1, d)), const((1, d)), const((d, 2 * D_FF))],
        out_specs=[tok(d), tok(D_FF), tok(D_FF)],
        out_shape=[jax.ShapeDtypeStruct((b, lt, d), F32), jax.ShapeDtypeStruct((b, lt, D_FF), BF16),
                   jax.ShapeDtypeStruct((b, lt, D_FF), BF16)],
        scratch_shapes=[pltpu.VMEM((rows, TB, MIX_W), BF16)],
        compiler_params=_cparams(("arbitrary", "arbitrary")),
    )(*h_ops, modtab, ys5[0], ys5[1], us5[0], us5[1], z, z, ogla[0], ogla[1], oml[0], oml[1], dsk, wglu, bglu,
      gg, mg, wout, l1g, l1b, wup)


FF_CW = 256
GELU_E0 = -2.0 * math.sqrt(2.0 / math.pi) * math.log2(math.e)
GELU_E1 = 0.044715 * GELU_E0


def _ffn_kernel(alpha, nb, h_ref, mod_ref, a_ref, ap_ref, an_ref, v_ref, wc_ref, bc_ref, wd_ref,
                l2g_ref, l2b_ref, o_ref, g_ref):
    j = pl.program_id(1)
    is_lat = j > 0
    ngrp = TB // GRID_W
    grp = lax.broadcasted_iota(jnp.int32, (ngrp, 8, 1), 0)
    sub = lax.broadcasted_iota(jnp.int32, (ngrp, 8, 1), 1)
    edge_l = jnp.where((sub == 0) & (is_lat | (grp == 0)), 0.0, 1.0)
    edge_r = jnp.where((sub == 7) & (is_lat | (grp == ngrp - 1)), 0.0, 1.0)

    def mask_group_tile(x, first, edge):
        x = x.reshape(ngrp, GRID_W, x.shape[1])
        if first:
            x = jnp.concatenate([x[:, :8] * edge, x[:, 8:]], axis=1)
        else:
            x = jnp.concatenate([x[:, :GRID_W - 8], x[:, GRID_W - 8:] * edge], axis=1)
        return x.reshape(TB, x.shape[2])

    vert = is_lat.astype(F32)
    up_ok = (j > 1).astype(BF16)
    dn_ok = (j < nb - 1).astype(BF16)
    for r in range(h_ref.shape[0]):
        for c0 in range(0, D_FF, FF_CW):
            cs = slice(c0, c0 + FF_CW)
            a = a_ref[r, :, cs]
            up = jnp.concatenate([ap_ref[r, :, cs] * up_ok, a[:TB - GRID_W]], axis=0)
            dn = jnp.concatenate([a[GRID_W:], an_ref[r, :, cs] * dn_ok], axis=0)
            w = wc_ref[:, cs]
            wb = w.astype(BF16)
            wvb = (w * vert).astype(BF16)
            left = (up * wvb[0:1] + a * wb[3:4] + dn * wvb[6:7]).astype(F32)
            mid = (up * wvb[1:2] + a * wb[4:5] + dn * wvb[7:8]).astype(F32)
            right = (up * wvb[2:3] + a * wb[5:6] + dn * wvb[8:9]).astype(F32)
            acc = ((mid + bc_ref[:, cs]) + mask_group_tile(pltpu.roll(left, 1, 0), True, edge_l)
                   + mask_group_tile(pltpu.roll(right, TB - 1, 0), False, edge_r))
            e = jnp.exp2(acc * (GELU_E0 + GELU_E1 * (acc * acc)))
            g_ref[r, :, cs] = (acc / (1.0 + e)).astype(BF16) * v_ref[r, :, cs]
    for r in range(h_ref.shape[0]):
        f = jnp.dot(g_ref[r], wd_ref[...], preferred_element_type=F32)
        y = alpha * h_ref[r] + mod_ref[r, 0][5:6] * f
        o_ref[r] = _ln_rows(y) * l2g_ref[...] + l2b_ref[...]


def _ffn(alpha, h1, modtab, a, v, wc, bc, wd, l2g, l2b, latent_only):
    b, lt, d = h1.shape
    nb = lt // TB
    rpb = TB // GRID_W
    nrow = lt // GRID_W
    rows = OUT_ROWS if b % OUT_ROWS == 0 else 1
    const = lambda shape: pl.BlockSpec(shape, lambda bi, j: (0,) * len(shape), pipeline_mode=pl.Buffered(1))
    tok = lambda w: pl.BlockSpec((rows, TB, w), lambda bi, j: (bi, j, 0))
    if latent_only:
        out_spec = pl.BlockSpec((rows, TB, d), lambda bi, j: (bi, jnp.maximum(j - 1, 0), 0))
        out_len = lt - TB
    else:
        out_spec, out_len = tok(d), lt
    return pl.pallas_call(
        functools.partial(_ffn_kernel, alpha, nb),
        grid=(b // rows, nb),
        in_specs=[tok(d), pl.BlockSpec((rows, 1, 6, d), lambda bi, j: (bi, jnp.minimum(j, 1), 0, 0)),
                  tok(D_FF),
                  pl.BlockSpec((rows, GRID_W, D_FF), lambda bi, j: (bi, jnp.maximum(j * rpb - 1, 0), 0)),
                  pl.BlockSpec((rows, GRID_W, D_FF),
                               lambda bi, j: (bi, jnp.minimum((j + 1) * rpb, nrow - 1), 0)),
                  tok(D_FF), const((9, D_FF)), const((1, D_FF)), const((D_FF, d)),
                  const((1, d)), const((1, d))],
        out_specs=out_spec,
        out_shape=jax.ShapeDtypeStruct((b, out_len, d), F32),
        scratch_shapes=[pltpu.VMEM((rows, TB, D_FF), BF16)],
        compiler_params=_cparams(("arbitrary", "arbitrary")),
    )(h1, modtab, a, a, a, v, wc, bc, wd, l2g, l2b)


def _pad_heads(w, nh, dh, width=LANE):
    lead = w.shape[:-1]
    w = w.reshape(lead + (nh, dh))
    w = jnp.pad(w, [(0, 0)] * len(lead) + [(0, 0), (0, width - dh)])
    return w.reshape(lead + (nh * width,))


def _pad_head_rows(w, nh, dh):
    return jnp.moveaxis(_pad_heads(jnp.moveaxis(w, 0, -1), nh, dh), -1, 0)


def _layer_weights(w_in, gla_w_a2, gla_b_a, ml_i_bias, ml_f_bias, w_out):
    d = w_in.shape[0]
    offs = np.cumsum((0, S5_W, GLA_H * GLA_DK, GLA_H * GLA_DK, GLA_W, GLA_W, 2 * GLA_RANK,
                      ML_W, ML_W, ML_W, ML_W, 2 * ML_H, 2 * ML_H))
    part = lambda i: w_in[:, offs[i]:offs[i + 1]]
    w_p = jnp.concatenate([
        _pad_heads(part(1), GLA_H, GLA_DK, GLA_PW), _pad_heads(part(2), GLA_H, GLA_DK, GLA_PW),
        _pad_heads(part(3), GLA_H, GLA_DV), _pad_heads(part(4), GLA_H, GLA_DV),
        _pad_heads(part(6), ML_H, ML_D), _pad_heads(part(7), ML_H, ML_D),
        _pad_heads(part(8), ML_H, ML_D), _pad_heads(part(9), ML_H, ML_D)], axis=1).astype(BF16)
    w_u = part(0).astype(BF16)
    lr, ig, fg = part(5), part(10), part(11)
    zeros = lambda n: jnp.zeros((d, n), w_in.dtype)
    tiles = []
    for dd in range(2):
        tiles += [lr[:, dd * GLA_RANK:(dd + 1) * GLA_RANK], ig[:, dd * ML_H:(dd + 1) * ML_H],
                  zeros(LANE - GLA_RANK - ML_H),
                  zeros(GATE_OFF), fg[:, dd * ML_H:(dd + 1) * ML_H], zeros(LANE - GATE_OFF - ML_H)]
    w_g = jnp.concatenate(tiles, axis=1).astype(BF16)
    rows_t = []
    for dd in range(2):
        for gate in (ig, fg):
            rows_t += [gate[:, dd * ML_H:(dd + 1) * ML_H], zeros(8 - ML_H)]
    w_gt = jnp.concatenate(rows_t, axis=1).T.astype(BF16)
    wa = jnp.pad(_pad_heads(gla_w_a2, GLA_H, GLA_DK, GLA_PW), ((0, 0), (0, LANE - GLA_RANK), (0, 0)))
    ba = _pad_heads(gla_b_a, GLA_H, GLA_DK, GLA_PW)[:, None, :]
    gate_row = lambda bias: jnp.pad(bias, ((0, 0), (GATE_OFF, LANE - GATE_OFF - ML_H)))[:, None, :]
    gate_col = lambda bias: jnp.pad(bias, ((0, 0), (0, 8 - ML_H)))[:, :, None]
    wout_p = jnp.concatenate([
        w_out[:S5_W], _pad_head_rows(w_out[S5_W:S5_W + GLA_W], GLA_H, GLA_DV),
        _pad_head_rows(w_out[S5_W + GLA_W:], ML_H, ML_D)], axis=0).astype(BF16)
    return dict(w_p=w_p, w_u=w_u, w_g=w_g, w_gt=w_gt, wa=wa, ba=ba,
                ibr=gate_row(ml_i_bias), fbr=gate_row(ml_f_bias),
                ibc=gate_col(ml_i_bias), fbc=gate_col(ml_f_bias), wout=wout_p)


def kernel(x, c, ctx, c_ctx, w_ada, b_ada, w_in, s5_a_re, s5_a_im, s5_log_dt, s5_b_re, s5_b_im, s5_c_re, s5_c_im, s5_d, s5_w_glu, s5_b_glu, gla_w_a2, gla_b_a, gla_g, ml_i_bias, ml_f_bias, ml_g, w_out, ln1_g, ln1_b, w_up, w_dconv, b_dconv, w_down, ln2_g, ln2_b):
    b, seq, d = x.shape
    ctx_len = ctx.shape[1]
    depth = w_in.shape[0]
    assert d == D_MODEL and ctx_len == TB and seq % TB == 0 and b <= 7
    alpha = (2.0 * depth) ** 0.25

    cc = jnp.zeros((8, d), F32).at[:b].set(c).at[b].set(c_ctx)
    mod = _modulation(cc, w_ada, b_ada).reshape(depth, 8, 6, d)
    modtab = jnp.stack([jnp.broadcast_to(mod[:, b:b + 1], (depth, b, 6, d)), mod[:, :b]], axis=2)

    eye = np.eye(NCB, dtype=np.float32)
    low = np.kron(eye, np.tril(np.ones((CHUNK, CHUNK), np.float32)))
    tri = (jnp.asarray(low), jnp.asarray(low.T))

    fold = lambda t: jnp.moveaxis(t, 0, 1).reshape((2, depth * S5_G) + t.shape[3:])
    s5_ops = _s5_operators(*(fold(t) for t in (s5_a_re, s5_a_im, s5_log_dt, s5_b_re, s5_b_im,
                                                s5_c_re, s5_c_im)))

    h = (ctx, x)
    for l in range(depth):
        lw = _layer_weights(w_in[l], gla_w_a2[l], gla_b_a[l], ml_i_bias[l], ml_f_bias[l], w_out[l])
        z, ua, ub, gates, gates_t = _in_proj(h, modtab[l], lw["w_p"], lw["w_u"], lw["w_g"], lw["w_gt"])
        ogla = _gla(z, gates, lw["wa"], lw["ba"], tri)
        oml = _mlstm(z, gates, gates_t, lw["ibr"], lw["fbr"], lw["ibc"], lw["fbc"], tri)
        ys5 = _s5(ua, ub, *(t[l * S5_G:(l + 1) * S5_G] for t in s5_ops))
        h1, a, v = _out_proj(alpha, h, modtab[l], ys5, (ua, ub), z, ogla, oml, s5_d[l][None],
                             s5_w_glu[l].astype(BF16),
                             s5_b_glu[l][None], _pad_heads(gla_g[l], GLA_H, GLA_DV)[None],
                             _pad_heads(ml_g[l], ML_H, ML_D)[None], lw["wout"], ln1_g[l][None],
                             ln1_b[l][None], w_up[l].astype(BF16))
        h = _ffn(alpha, h1, modtab[l], a, v, w_dconv[l].reshape(9, D_FF), b_dconv[l][None],
                 w_down[l].astype(BF16), ln2_g[l][None], ln2_b[l][None], latent_only=(l == depth - 1))
    return h
```
